```python
import math, functools
import jax, jax.numpy as jnp
from jax import lax
import numpy as np

D_MODEL = 1024
BATCH = 16
SEQ = 2048
DEPTH = 2
DEC_BATCH = 8
DEC_SEQ = 16
PAST_LEN = 2048

CHUNK = 64
Q_BLOCK = 128
N_MIXERS = 2
H_A = 8
DH_A = 64
DV_A = 2 * DH_A
D_A = H_A * DV_A
H_B = 16
DH_B = 64
D_B = H_B * DH_B
RMS_EPS = 1e-6

kernel_name = 'streaming_diff_stickbreak_hybrid_step'


def rms_norm(x, g):
    x32 = x.astype(jnp.float32)
    y = x32 * lax.rsqrt(jnp.mean(x32 * x32, axis=-1, keepdims=True) + RMS_EPS)
    return (y * g.astype(jnp.float32)).astype(x.dtype)


def alibi_slopes(n_heads):
    return 2.0 ** (-8.0 * jnp.arange(1, n_heads + 1, dtype=jnp.float32) / n_heads)


def diff_lambda_init(layer):
    return 0.8 - 0.6 * math.exp(-0.3 * layer)


def diff_core(q, k, v, q_pos, k_pos, lam, slopes):
    s = jnp.einsum('bqhcd,bkhcd->bchqk', q.astype(jnp.float32), k.astype(jnp.float32)) * (DH_A ** -0.5)
    dist = jnp.abs(q_pos[:, None] - k_pos[None, :]).astype(jnp.float32)
    visible = (k_pos[None, :] // CHUNK) <= (q_pos[:, None] // CHUNK)
    s = jnp.where(visible, s - slopes[:, None, None] * dist, -jnp.inf)
    p = jax.nn.softmax(s, axis=-1)
    a = p[:, 0] - lam * p[:, 1]
    return jnp.einsum('bhqk,bkhe->bqhe', a, v.astype(jnp.float32))


def stick_breaking_core(q, k, v, q_pos, k_pos):
    z = jnp.einsum('bqhd,bkhd->bhqk', q.astype(jnp.float32), k.astype(jnp.float32)) * (DH_B ** -0.5)
    before = k_pos[None, :] < q_pos[:, None]
    log_beta = jax.nn.log_sigmoid(z)
    log_keep = jnp.where(before, log_beta - z, 0.0)
    log_keep_after = lax.cumsum(log_keep, axis=3, reverse=True) - log_keep
    a = jnp.where(before, jnp.exp(log_beta + log_keep_after), 0.0)
    return jnp.einsum('bhqk,bkhd->bqhd', a, v.astype(jnp.float32))


def sweep_query_blocks(core, q, k, v):
    seq = q.shape[1]
    outs = []
    for i in range(seq // Q_BLOCK):
        lo, hi = i * Q_BLOCK, (i + 1) * Q_BLOCK
        outs.append(core(q[:, lo:hi], k[:, :hi], v[:, :hi], jnp.arange(lo, hi), jnp.arange(hi)))
    return jnp.concatenate(outs, axis=1)


def diff_layer(x, norm_g, w_in, q_norm, k_norm, lambda_q1, lambda_k1, lambda_q2, lambda_k2,
               subln_g, w_out, layer, past_k=None, past_v=None):
    b, s, _ = x.shape
    h = rms_norm(x, norm_g)
    q, k, v, gate = jnp.split(h @ w_in, [D_A, 2 * D_A, 3 * D_A], axis=-1)
    q = rms_norm(q.reshape(b, s, H_A, 2, DH_A), q_norm)
    k = rms_norm(k.reshape(b, s, H_A, 2, DH_A), k_norm)
    v = v.reshape(b, s, H_A, DV_A)
    lam_init = diff_lambda_init(layer)
    f32 = jnp.float32
    lam = (jnp.exp(jnp.sum(lambda_q1.astype(f32) * lambda_k1.astype(f32)))
           - jnp.exp(jnp.sum(lambda_q2.astype(f32) * lambda_k2.astype(f32))) + lam_init)
    core = functools.partial(diff_core, lam=lam, slopes=alibi_slopes(H_A))
    if past_k is None:
        o = sweep_query_blocks(core, q, k, v)
    else:
        p_len = past_k.shape[1]
        k_all = jnp.concatenate([past_k.reshape(b, p_len, H_A, 2, DH_A), k], axis=1)
        v_all = jnp.concatenate([past_v, v], axis=1)
        o = core(q, k_all, v_all, p_len + jnp.arange(s), jnp.arange(p_len + s))
    o = rms_norm(o, subln_g) * (1.0 - lam_init)
    y = (o.reshape(b, s, D_A).astype(x.dtype) * jax.nn.silu(gate)) @ w_out
    return x + y, k.reshape(b, s, H_A, 2 * DH_A), v


def stick_breaking_layer(x, norm_g, w_in, w_out, past_k=None, past_v=None):
    b, s, _ = x.shape
    h = rms_norm(x, norm_g)
    q, k, v, gate = jnp.split(h @ w_in, [D_B, 2 * D_B, 3 * D_B], axis=-1)
    q = q.reshape(b, s, H_B, DH_B)
    k = k.reshape(b, s, H_B, DH_B)
    v = v.reshape(b, s, H_B, DH_B)
    if past_k is None:
        o = sweep_query_blocks(stick_breaking_core, q, k, v)
    else:
        p_len = past_k.shape[1]
        k_all = jnp.concatenate([past_k, k], axis=1)
        v_all = jnp.concatenate([past_v, v], axis=1)
        o = stick_breaking_core(q, k_all, v_all, p_len + jnp.arange(s), jnp.arange(p_len + s))
    y = (o.reshape(b, s, D_B).astype(x.dtype) * jax.nn.silu(gate)) @ w_out
    return x + y, k, v


def setup_inputs(seed: int = 0) -> dict:
    key = jax.random.key(seed)
    ks = jax.random.split(key, 20)
    f32 = jnp.float32
    nrm = lambda k, shape, scale: scale * jax.random.normal(k, shape, f32)
    return {
        'x_prompt': nrm(ks[0], (BATCH, SEQ, D_MODEL), 1.0),
        'x_sample': nrm(ks[1], (DEC_BATCH, DEC_SEQ, D_MODEL), 1.0),
        'cache_k_0': nrm(ks[2], (DEC_BATCH, PAST_LEN, H_A, 2 * DH_A), 1.0),
        'cache_v_0': nrm(ks[3], (DEC_BATCH, PAST_LEN, H_A, DV_A), 1.0),
        'cache_k_1': nrm(ks[4], (DEC_BATCH, PAST_LEN, H_B, DH_B), 1.0),
        'cache_v_1': nrm(ks[5], (DEC_BATCH, PAST_LEN, H_B, DH_B), 1.0),
        'norm_g_0': 1.0 + nrm(ks[6], (D_MODEL,), 0.02),
        'w_in_0': nrm(ks[7], (D_MODEL, 4 * D_A), D_MODEL ** -0.5),
        'q_norm_0': 1.0 + nrm(ks[8], (DH_A,), 0.02),
        'k_norm_0': 1.0 + nrm(ks[9], (DH_A,), 0.02),
        'lambda_q1_0': nrm(ks[10], (DH_A,), 0.1),
        'lambda_k1_0': nrm(ks[11], (DH_A,), 0.1),
        'lambda_q2_0': nrm(ks[12], (DH_A,), 0.1),
        'lambda_k2_0': nrm(ks[13], (DH_A,), 0.1),
        'subln_g_0': 1.0 + nrm(ks[14], (DV_A,), 0.02),
        'w_out_0': nrm(ks[15], (D_A, D_MODEL), D_A ** -0.5),
        'norm_g_1': 1.0 + nrm(ks[16], (D_MODEL,), 0.02),
        'w_in_1': nrm(ks[17], (D_MODEL, 4 * D_B), D_MODEL ** -0.5),
        'w_out_1': nrm(ks[18], (D_B, D_MODEL), D_B ** -0.5),
    }


def reference(x_prompt, x_sample, cache_k_0, cache_v_0, cache_k_1, cache_v_1,
              norm_g_0, w_in_0, q_norm_0, k_norm_0, lambda_q1_0, lambda_k1_0,
              lambda_q2_0, lambda_k2_0, subln_g_0, w_out_0,
              norm_g_1, w_in_1, w_out_1):
    diff_params = (norm_g_0, w_in_0, q_norm_0, k_norm_0, lambda_q1_0, lambda_k1_0,
                   lambda_q2_0, lambda_k2_0, subln_g_0, w_out_0)
    sb_params = (norm_g_1, w_in_1, w_out_1)
    layer_inputs = ((diff_params, cache_k_0, cache_v_0), (sb_params, cache_k_1, cache_v_1))
    y_prompt, y_sample = x_prompt, x_sample
    new_state = []
    for layer in range(DEPTH):
        params, ck, cv = layer_inputs[layer]
        if layer % N_MIXERS == 0:
            y_prompt, kp, vp = diff_layer(y_prompt, *params, layer=layer)
            y_sample, ks, vs = diff_layer(y_sample, *params, layer=layer, past_k=ck, past_v=cv)
        else:
            y_prompt, kp, vp = stick_breaking_layer(y_prompt, *params)
            y_sample, ks, vs = stick_breaking_layer(y_sample, *params, past_k=ck, past_v=cv)
        new_state.append((kp, vp, ks, vs))
    (k0_prompt, v0_prompt, k0_sample, v0_sample), (k1_prompt, v1_prompt, k1_sample, v1_sample) = new_state
    return (y_prompt, y_sample, k0_prompt, v0_prompt, k0_sample, v0_sample,
            k1_prompt, v1_prompt, k1_sample, v1_sample)
```

```python
import functools
import math

import jax
import jax.numpy as jnp
from jax import lax
from jax.experimental import pallas as pl
from jax.experimental.pallas import tpu as pltpu

F32 = jnp.float32
BF16 = jnp.bfloat16

D_MODEL = 1024
CHUNK = 64
H_A = 8
DH = 64
HEAD_COLS = 128
N_GROUPS = D_MODEL // HEAD_COLS
RMS_EPS = 1e-6
NEG_BIG = -1e30
QK_SCALE = DH ** -0.5

TQ = 256
TK = 256
NEW_PAD = 128

VMEM_LIMIT = 48 * 1024 * 1024


def _cparams(n_axes):
    return pltpu.CompilerParams(
        dimension_semantics=("arbitrary",) * n_axes, vmem_limit_bytes=VMEM_LIMIT)


def _dot(a, b):
    return jnp.dot(a, b, preferred_element_type=F32)


def _dot_nt(a, b):
    return lax.dot_general(a, b, (((1,), (1,)), ((), ())), preferred_element_type=F32)


def _group_rms(t, gain_ref, bd_ref):
    sq = (t * t).astype(BF16)
    outs = []
    for c in range(0, D_MODEL, 256):
        ms = _dot(sq[:, c:c + 256], bd_ref[...])
        outs.append(t[:, c:c + 256] * lax.rsqrt(ms + RMS_EPS) * gain_ref[:, c:c + 256])
    return jnp.concatenate(outs, axis=1)


def _inproj_kernel(*refs, qk_norm):
    if qk_norm:
        (x_ref, g_ref, w_ref, qg_ref, kg_ref, bd_ref,
         q_ref, k_ref, v_ref, kb_ref, vb_ref, sg_ref) = refs
    else:
        x_ref, g_ref, w_ref, q_ref, k_ref, v_ref, kb_ref, vb_ref, sg_ref = refs
    x = x_ref[...]
    ms = jnp.mean(x * x, axis=-1, keepdims=True)
    h = (x * lax.rsqrt(ms + RMS_EPS) * g_ref[...]).astype(BF16)

    q = _dot(h, w_ref[:, 0:D_MODEL])
    if qk_norm:
        q = _group_rms(q, qg_ref, bd_ref)
    q_ref[...] = (q * QK_SCALE).astype(BF16)

    k = _dot(h, w_ref[:, D_MODEL:2 * D_MODEL])
    if qk_norm:
        k = _group_rms(k, kg_ref, bd_ref)
    k_ref[...] = k
    kb_ref[...] = k.astype(BF16)

    v = _dot(h, w_ref[:, 2 * D_MODEL:3 * D_MODEL])
    v_ref[...] = v
    vb_ref[...] = v.astype(BF16)

    gate = _dot(h, w_ref[:, 3 * D_MODEL:4 * D_MODEL])
    sg_ref[...] = (gate / (1.0 + jnp.exp(-gate))).astype(BF16)


def _in_proj(x2d, norm_g, w_bf16, qk_gains, tm):
    n = x2d.shape[0]
    row = lambda i: (i, 0)
    fixed = lambda i: (0, 0)
    in_specs = [pl.BlockSpec((tm, D_MODEL), row),
                pl.BlockSpec((1, D_MODEL), fixed),
                pl.BlockSpec((D_MODEL, 4 * D_MODEL), fixed)]
    args = [x2d, norm_g.reshape(1, D_MODEL), w_bf16]
    if qk_gains is not None:
        q_gain, k_gain = qk_gains
        r = jnp.arange(256)
        bd = jnp.where((r[:, None] // DH) == (r[None, :] // DH), 1.0 / DH, 0.0).astype(BF16)
        in_specs += [pl.BlockSpec((1, D_MODEL), fixed), pl.BlockSpec((1, D_MODEL), fixed),
                     pl.BlockSpec((256, 256), fixed)]
        args += [jnp.tile(q_gain.astype(F32), D_MODEL // DH).reshape(1, D_MODEL),
                 jnp.tile(k_gain.astype(F32), D_MODEL // DH).reshape(1, D_MODEL), bd]
    out_block = pl.BlockSpec((tm, D_MODEL), row)
    shp = lambda dt: jax.ShapeDtypeStruct((n, D_MODEL), dt)
    return pl.pallas_call(
        functools.partial(_inproj_kernel, qk_norm=qk_gains is not None),
        grid=(n // tm,),
        in_specs=in_specs,
        out_specs=[out_block] * 6,
        out_shape=[shp(BF16), shp(F32), shp(F32), shp(BF16), shp(BF16), shp(BF16)],
        compiler_params=_cparams(1),
        name="in_proj_qknorm" if qk_gains is not None else "in_proj",
    )(*args)


def _outproj_kernel(x_ref, a_ref, w_ref, y_ref):
    y_ref[...] = x_ref[...] + _dot(a_ref[...], w_ref[...])


def _out_proj(x2d, a_bf16, w_bf16, tm):
    n = x2d.shape[0]
    row = lambda i: (i, 0)
    return pl.pallas_call(
        _outproj_kernel,
        grid=(n // tm,),
        in_specs=[pl.BlockSpec((tm, D_MODEL), row), pl.BlockSpec((tm, D_MODEL), row),
                  pl.BlockSpec((D_MODEL, D_MODEL), lambda i: (0, 0))],
        out_specs=pl.BlockSpec((tm, D_MODEL), row),
        out_shape=jax.ShapeDtypeStruct((n, D_MODEL), F32),
        compiler_params=_cparams(1),
        name="out_proj",
    )(x2d, a_bf16, w_bf16)


def _split_halves(q):
    lane = lax.broadcasted_iota(jnp.int32, q.shape, 1)
    zero = jnp.zeros_like(q)
    return jnp.concatenate([jnp.where(lane < DH, q, zero), jnp.where(lane >= DH, q, zero)], axis=0)


def _softmax_step(s, shift, v, m_ref, l_ref, acc_ref):
    m_prev = m_ref[...]
    m_new = jnp.maximum(m_prev, jnp.max(s, axis=1, keepdims=True) - shift)
    p = jnp.exp(s - (m_new + shift))
    alpha = jnp.exp(m_prev - m_new)
    l_ref[...] = alpha * l_ref[...] + jnp.sum(p, axis=1, keepdims=True)
    acc_ref[...] = alpha * acc_ref[...] + _dot(p.astype(BF16), v)
    m_ref[...] = m_new


def _diff_lambda(lam_ref, lam_init):
    t1 = jnp.sum(lam_ref[0:1, :] * lam_ref[1:2, :], axis=-1, keepdims=True)
    t2 = jnp.sum(lam_ref[2:3, :] * lam_ref[3:4, :], axis=-1, keepdims=True)
    return jnp.exp(t1) - jnp.exp(t2) + lam_init


def _diff_finish(t, lam_ref, subg_ref, sg_ref, l_ref, acc_ref, o_ref, lam_init):
    lam = _diff_lambda(lam_ref, lam_init)
    inv_l = 1.0 / l_ref[...]
    o = acc_ref[0:t, :] * inv_l[0:t] - lam * (acc_ref[t:2 * t, :] * inv_l[t:2 * t])
    ms = jnp.mean(o * o, axis=-1, keepdims=True)
    on = o * lax.rsqrt(ms + RMS_EPS) * subg_ref[...] * (1.0 - lam_init)
    o_ref[0] = (on * sg_ref[0].astype(F32)).astype(BF16)


def _sb_terms(z):
    lp = jnp.log(1.0 + jnp.exp(-jnp.abs(z)))
    log_beta = jnp.minimum(z, 0.0) - lp
    return log_beta, log_beta - z


def _suffix_sums(log_keep, u):
    hi = log_keep.astype(BF16)
    lo = (log_keep - hi.astype(F32)).astype(BF16)
    return _dot(hi, u) + _dot(lo, u)


def _sb_finish(t, sg_ref, acc_ref, o_ref):
    lane = lax.broadcasted_iota(jnp.int32, (t, HEAD_COLS), 1)
    o = jnp.where(lane < DH, acc_ref[0:t, :], acc_ref[t:2 * t, :])
    o_ref[0] = (o * sg_ref[0].astype(F32)).astype(BF16)


def _diff_prompt_kernel(slope_ref, lam_ref, subg_ref, boff_ref, bdiag_ref,
                        q_ref, k_ref, v_ref, sg_ref, o_ref,
                        m_ref, l_ref, acc_ref, *, lam_init):
    hh = pl.program_id(1)
    qi = pl.program_id(2)
    slope = slope_ref[hh]
    qq = _split_halves(q_ref[0])

    m_ref[...] = jnp.full(m_ref.shape, NEG_BIG, F32)
    l_ref[...] = jnp.zeros(l_ref.shape, F32)
    acc_ref[...] = jnp.zeros(acc_ref.shape, F32)

    def block(kj, bias_ref, shift):
        start = pl.multiple_of(kj * TK, TK)
        k = k_ref[0, pl.ds(start, TK), :]
        v = v_ref[0, pl.ds(start, TK), :]
        s = _dot_nt(qq, k)
        bias = bias_ref[0]
        s = jnp.concatenate([s[0:TQ] - bias, s[TQ:2 * TQ] - bias], axis=0)
        _softmax_step(s, shift, v, m_ref, l_ref, acc_ref)

    def past(kj, carry):
        block(kj, boff_ref, slope * ((qi - kj) * TQ).astype(F32))
        return carry

    lax.fori_loop(0, qi, past, 0)
    block(qi, bdiag_ref, 0.0)
    _diff_finish(TQ, lam_ref, subg_ref, sg_ref, l_ref, acc_ref, o_ref, lam_init)


def _sb_prompt_kernel(u_ref, q_ref, k_ref, v_ref, sg_ref, o_ref, c_ref, acc_ref):
    qi = pl.program_id(2)
    qq = _split_halves(q_ref[0])
    c_ref[...] = jnp.zeros(c_ref.shape, F32)
    acc_ref[...] = jnp.zeros(acc_ref.shape, F32)

    def block(kj, diagonal):
        start = pl.multiple_of(kj * TK, TK)
        k = k_ref[0, pl.ds(start, TK), :]
        v = v_ref[0, pl.ds(start, TK), :]
        log_beta, log_keep = _sb_terms(_dot_nt(qq, k))
        if diagonal:
            row = lax.broadcasted_iota(jnp.int32, (2 * TQ, TK), 0) & (TQ - 1)
            col = lax.broadcasted_iota(jnp.int32, (2 * TQ, TK), 1)
            before = col < row
            log_keep = jnp.where(before, log_keep, 0.0)
        a = jnp.exp(log_beta + _suffix_sums(log_keep, u_ref[...]) + c_ref[...])
        if diagonal:
            a = jnp.where(before, a, 0.0)
        acc_ref[...] += _dot(a.astype(BF16), v)
        c_ref[...] += jnp.sum(log_keep, axis=1, keepdims=True)

    block(qi, True)

    def earlier(i, carry):
        block(qi - 1 - i, False)
        return carry

    lax.fori_loop(0, qi, earlier, 0)
    _sb_finish(TQ, sg_ref, acc_ref, o_ref)


def _suffix_matrix(n):
    r = jnp.arange(n)
    return (r[:, None] > r[None, :]).astype(BF16)


def _alibi_slopes():
    return 2.0 ** (-8.0 * jnp.arange(1, H_A + 1, dtype=F32) / H_A)


def _diff_prompt_attention(q, kb, vb, sg, lam_rows, subln_g, lam_init):
    b, s, _ = q.shape
    slopes = _alibi_slopes()
    i = jnp.arange(TQ)
    rel = (i[:, None] - i[None, :]).astype(F32)
    visible = (i[None, :] // CHUNK) <= (i[:, None] // CHUNK)
    bias_off = slopes[:, None, None] * rel[None]
    bias_diag = jnp.where(visible[None], slopes[:, None, None] * jnp.abs(rel)[None], -NEG_BIG)
    qblk = pl.BlockSpec((1, TQ, HEAD_COLS), lambda bi, h, qi: (bi, qi, h))
    kvblk = pl.BlockSpec((1, s, HEAD_COLS), lambda bi, h, qi: (bi, 0, h))
    biasblk = pl.BlockSpec((1, TQ, TK), lambda bi, h, qi: (h, 0, 0))
    fixed2 = lambda bi, h, qi: (0, 0)
    return pl.pallas_call(
        functools.partial(_diff_prompt_kernel, lam_init=lam_init),
        grid=(b, N_GROUPS, s // TQ),
        in_specs=[pl.BlockSpec(memory_space=pltpu.SMEM),
                  pl.BlockSpec((8, HEAD_COLS), fixed2), pl.BlockSpec((1, HEAD_COLS), fixed2),
                  biasblk, biasblk, qblk, kvblk, kvblk, qblk],
        out_specs=qblk,
        out_shape=jax.ShapeDtypeStruct((b, s, D_MODEL), BF16),
        scratch_shapes=[pltpu.VMEM((2 * TQ, 1), F32), pltpu.VMEM((2 * TQ, 1), F32),
                        pltpu.VMEM((2 * TQ, HEAD_COLS), F32)],
        compiler_params=_cparams(3),
        name="diff_attn_prompt",
    )(slopes, lam_rows, subln_g.reshape(1, HEAD_COLS).astype(F32), bias_off, bias_diag, q, kb, vb, sg)


def _sb_prompt_attention(q, kb, vb, sg):
    b, s, _ = q.shape
    qblk = pl.BlockSpec((1, TQ, HEAD_COLS), lambda bi, h, qi: (bi, qi, h))
    kvblk = pl.BlockSpec((1, s, HEAD_COLS), lambda bi, h, qi: (bi, 0, h))
    return pl.pallas_call(
        _sb_prompt_kernel,
        grid=(b, N_GROUPS, s // TQ),
        in_specs=[pl.BlockSpec((TK, TK), lambda bi, h, qi: (0, 0)), qblk, kvblk, kvblk, qblk],
        out_specs=qblk,
        out_shape=jax.ShapeDtypeStruct((b, s, D_MODEL), BF16),
        scratch_shapes=[pltpu.VMEM((2 * TQ, 1), F32), pltpu.VMEM((2 * TQ, HEAD_COLS), F32)],
        compiler_params=_cparams(3),
        name="sb_attn_prompt",
    )(_suffix_matrix(TK), q, kb, vb, sg)


def _diff_decode_kernel(slope_ref, lam_ref, subg_ref, q_ref, ck_ref, cv_ref, nk_ref, nv_ref, sg_ref,
                        o_ref, m_ref, l_ref, acc_ref, *, lam_init, past_len, n_new):
    hh = pl.program_id(1)
    slope = slope_ref[hh]
    t = q_ref.shape[1]
    qq = _split_halves(q_ref[0])
    m_ref[...] = jnp.full(m_ref.shape, NEG_BIG, F32)
    l_ref[...] = jnp.zeros(l_ref.shape, F32)
    acc_ref[...] = jnp.zeros(acc_ref.shape, F32)

    def positions(n_keys, key0):
        q_pos = past_len + (lax.broadcasted_iota(jnp.int32, (2 * t, n_keys), 0) & (t - 1))
        k_pos = key0 + lax.broadcasted_iota(jnp.int32, (2 * t, n_keys), 1)
        return q_pos, k_pos

    def scores(k, q_pos, k_pos, valid):
        dist = jnp.abs(q_pos - k_pos).astype(F32)
        visible = valid & ((k_pos // CHUNK) <= (q_pos // CHUNK))
        return jnp.where(visible, _dot_nt(qq, k) - slope * dist, NEG_BIG)

    q_pos, k_pos = positions(past_len, 0)
    s = scores(ck_ref[0].astype(BF16), q_pos, k_pos, k_pos < past_len)
    _softmax_step(s, 0.0, cv_ref[0].astype(BF16), m_ref, l_ref, acc_ref)

    q_pos, k_pos = positions(NEW_PAD, past_len)
    s = scores(nk_ref[0], q_pos, k_pos, k_pos < past_len + n_new)
    _softmax_step(s, 0.0, nv_ref[0], m_ref, l_ref, acc_ref)
    _diff_finish(t, lam_ref, subg_ref, sg_ref, l_ref, acc_ref, o_ref, lam_init)


def _sb_decode_kernel(u_ref, q_ref, ck_ref, cv_ref, nk_ref, nv_ref, sg_ref, o_ref, acc_ref,
                      *, past_len, n_new):
    t = q_ref.shape[1]
    qq = _split_halves(q_ref[0])

    row = lax.broadcasted_iota(jnp.int32, (2 * t, NEW_PAD), 0) & (t - 1)
    col = lax.broadcasted_iota(jnp.int32, (2 * t, NEW_PAD), 1)
    before = (col < row) & (col < n_new)
    log_beta, log_keep = _sb_terms(_dot_nt(qq, nk_ref[0]))
    log_keep = jnp.where(before, log_keep, 0.0)
    a = jnp.exp(log_beta + _suffix_sums(log_keep, u_ref[0:NEW_PAD, 0:NEW_PAD]))
    a = jnp.where(before, a, 0.0)
    acc_ref[...] = _dot(a.astype(BF16), nv_ref[0])
    carry = jnp.sum(log_keep, axis=1, keepdims=True)

    for blk in range(past_len // TK - 1, -1, -1):
        k = ck_ref[0, blk * TK:(blk + 1) * TK, :].astype(BF16)
        v = cv_ref[0, blk * TK:(blk + 1) * TK, :].astype(BF16)
        log_beta, log_keep = _sb_terms(_dot_nt(qq, k))
        a = jnp.exp(log_beta + _suffix_sums(log_keep, u_ref[...]) + carry)
        acc_ref[...] += _dot(a.astype(BF16), v)
        carry = carry + jnp.sum(log_keep, axis=1, keepdims=True)
    _sb_finish(t, sg_ref, acc_ref, o_ref)


def _pad_new(x):
    return jnp.pad(x, ((0, 0), (0, NEW_PAD - x.shape[1]), (0, 0)))


def _decode_specs(t, past_len):
    grp = lambda bi, h: (bi, 0, h)
    qblk = pl.BlockSpec((1, t, HEAD_COLS), grp)
    cblk = pl.BlockSpec((1, past_len, HEAD_COLS), grp)
    nblk = pl.BlockSpec((1, NEW_PAD, HEAD_COLS), grp)
    return qblk, cblk, nblk


def _diff_decode_attention(q, cache_k, cache_v, kb_new, vb_new, sg, lam_rows, subln_g, lam_init):
    b, t, _ = q.shape
    past_len = cache_k.shape[1]
    qblk, cblk, nblk = _decode_specs(t, past_len)
    fixed2 = lambda bi, h: (0, 0)
    return pl.pallas_call(
        functools.partial(_diff_decode_kernel, lam_init=lam_init, past_len=past_len, n_new=t),
        grid=(b, N_GROUPS),
        in_specs=[pl.BlockSpec(memory_space=pltpu.SMEM),
                  pl.BlockSpec((8, HEAD_COLS), fixed2), pl.BlockSpec((1, HEAD_COLS), fixed2),
                  qblk, cblk, cblk, nblk, nblk, qblk],
        out_specs=qblk,
        out_shape=jax.ShapeDtypeStruct((b, t, D_MODEL), BF16),
        scratch_shapes=[pltpu.VMEM((2 * t, 1), F32), pltpu.VMEM((2 * t, 1), F32),
                        pltpu.VMEM((2 * t, HEAD_COLS), F32)],
        compiler_params=_cparams(2),
        name="diff_attn_decode",
    )(_alibi_slopes(), lam_rows, subln_g.reshape(1, HEAD_COLS).astype(F32),
      q, cache_k, cache_v, _pad_new(kb_new), _pad_new(vb_new), sg)


def _sb_decode_attention(q, cache_k, cache_v, kb_new, vb_new, sg):
    b, t, _ = q.shape
    past_len = cache_k.shape[1]
    qblk, cblk, nblk = _decode_specs(t, past_len)
    return pl.pallas_call(
        functools.partial(_sb_decode_kernel, past_len=past_len, n_new=t),
        grid=(b, N_GROUPS),
        in_specs=[pl.BlockSpec((TK, TK), lambda bi, h: (0, 0)), qblk, cblk, cblk, nblk, nblk, qblk],
        out_specs=qblk,
        out_shape=jax.ShapeDtypeStruct((b, t, D_MODEL), BF16),
        scratch_shapes=[pltpu.VMEM((2 * t, HEAD_COLS), F32)],
        compiler_params=_cparams(2),
        name="sb_attn_decode",
    )(_suffix_matrix(TK), q, cache_k, cache_v, _pad_new(kb_new), _pad_new(vb_new), sg)


def _layer(x, cache_k, cache_v, norm_g, w_in, w_out, qk_gains, attend_prompt, attend_decode):
    b, s, _ = x.shape
    n = b * s
    x2d = x.reshape(n, D_MODEL)
    q, k, v, kb, vb, sg = _in_proj(x2d, norm_g, w_in, qk_gains, tm=min(n, 256))
    as3 = lambda a: a.reshape(b, s, D_MODEL)
    if cache_k is None:
        a = attend_prompt(as3(q), as3(kb), as3(vb), as3(sg))
    else:
        past = cache_k.shape[1]
        a = attend_decode(as3(q), cache_k.reshape(b, past, D_MODEL), cache_v.reshape(b, past, D_MODEL),
                          as3(kb), as3(vb), as3(sg))
    y = _out_proj(x2d, a.reshape(n, D_MODEL), w_out, tm=min(n, 512))
    return as3(y), k, v


def kernel(x_prompt, x_sample, cache_k_0, cache_v_0, cache_k_1, cache_v_1, norm_g_0, w_in_0, q_norm_0, k_norm_0, lambda_q1_0, lambda_k1_0, lambda_q2_0, lambda_k2_0, subln_g_0, w_out_0, norm_g_1, w_in_1, w_out_1):
    bp, sp, _ = x_prompt.shape
    bs, ss, _ = x_sample.shape

    lam_init = 0.8 - 0.6 * math.exp(-0.3 * 0)
    lam_rows = jnp.zeros((8, HEAD_COLS), F32)
    for r, vec in enumerate((lambda_q1_0, lambda_k1_0, lambda_q2_0, lambda_k2_0)):
        lam_rows = lam_rows.at[r, 0:DH].set(vec.astype(F32))

    diff_prompt = functools.partial(_diff_prompt_attention, lam_rows=lam_rows, subln_g=subln_g_0,
                                    lam_init=lam_init)
    diff_decode = functools.partial(_diff_decode_attention, lam_rows=lam_rows, subln_g=subln_g_0,
                                    lam_init=lam_init)
    w_in_0b, w_out_0b = w_in_0.astype(BF16), w_out_0.astype(BF16)
    w_in_1b, w_out_1b = w_in_1.astype(BF16), w_out_1.astype(BF16)
    gains0 = (q_norm_0, k_norm_0)

    yp, k0p, v0p = _layer(x_prompt, None, None, norm_g_0, w_in_0b, w_out_0b, gains0, diff_prompt, diff_decode)
    ys, k0s, v0s = _layer(x_sample, cache_k_0, cache_v_0, norm_g_0, w_in_0b, w_out_0b, gains0,
                          diff_prompt, diff_decode)
    yp, k1p, v1p = _layer(yp, None, None, norm_g_1, w_in_1b, w_out_1b, None,
                          _sb_prompt_attention, _sb_decode_attention)
    ys, k1s, v1s = _layer(ys, cache_k_1, cache_v_1, norm_g_1, w_in_1b, w_out_1b, None,
                          _sb_prompt_attention, _sb_decode_attention)

    return (yp, ys,
            k0p.reshape(bp, sp, H_A, 2 * DH), v0p.reshape(bp, sp, H_A, 2 * DH),
            k0s.reshape(bs, ss, H_A, 2 * DH), v0s.reshape(bs, ss, H_A, 2 * DH),
            k1p.reshape(bp, sp, 2 * H_A, DH), v1p.reshape(bp, sp, 2 * H_A, DH),
            k1s.reshape(bs, ss, 2 * H_A, DH), v1s.reshape(bs, ss, 2 * H_A, DH))
```

```python
import functools
import math

import jax
import jax.numpy as jnp
from jax import lax
from jax.experimental import pallas as pl
from jax.experimental.pallas import tpu as pltpu

F32 = jnp.float32
BF16 = jnp.bfloat16

D_MODEL = 1024
CHUNK = 64
H_A = 8
DH = 64
HEAD_COLS = 128
N_GROUPS = D_MODEL // HEAD_COLS
RMS_EPS = 1e-6
NEG_BIG = -1e30
QK_SCALE = DH ** -0.5
LOG2_E = math.log2(math.e)

TQ = 256
TK = 256
NEW_PAD = 128
SB_GROUPS = 2
DIFF_GROUPS = 4

VMEM_LIMIT = 48 * 1024 * 1024


def _cparams(n_axes):
    return pltpu.CompilerParams(
        dimension_semantics=("arbitrary",) * n_axes, vmem_limit_bytes=VMEM_LIMIT)


def _dot(a, b):
    return jnp.dot(a, b, preferred_element_type=F32)


def _dot_nt(a, b):
    return lax.dot_general(a, b, (((1,), (1,)), ((), ())), preferred_element_type=F32)


def _group_rms(t, gain_ref, bd_ref):
    sq = (t * t).astype(BF16)
    outs = []
    for c in range(0, D_MODEL, 256):
        ms = _dot(sq[:, c:c + 256], bd_ref[...])
        outs.append(t[:, c:c + 256] * lax.rsqrt(ms + RMS_EPS) * gain_ref[:, c:c + 256])
    return jnp.concatenate(outs, axis=1)


def _inproj_kernel(*refs, qk_norm, v_blocks, q_scale):
    if qk_norm:
        (x_ref, g_ref, w_ref, qg_ref, kg_ref, bd_ref,
         q_ref, k_ref, v_ref, kb_ref, vb_ref, sg_ref) = refs
    else:
        x_ref, g_ref, w_ref, q_ref, k_ref, v_ref, kb_ref, vb_ref, sg_ref = refs
    x = x_ref[0]
    ms = jnp.mean(x * x, axis=-1, keepdims=True)
    h = (x * lax.rsqrt(ms + RMS_EPS) * g_ref[...]).astype(BF16)

    q = _dot(h, w_ref[:, 0:D_MODEL])
    if qk_norm:
        q = _group_rms(q, qg_ref, bd_ref)
    q_ref[0] = (q * q_scale).astype(BF16)

    k = _dot(h, w_ref[:, D_MODEL:2 * D_MODEL])
    if qk_norm:
        k = _group_rms(k, kg_ref, bd_ref)
    k_ref[0] = k
    kb_ref[0] = k.astype(BF16)

    v = _dot(h, w_ref[:, 2 * D_MODEL:3 * D_MODEL])
    v_ref[0] = v
    if v_blocks:
        for g in range(N_GROUPS):
            vb_ref[0, g, 0] = v[:, g * HEAD_COLS:(g + 1) * HEAD_COLS].T.astype(BF16)
    else:
        vb_ref[0] = v.astype(BF16)

    gate = _dot(h, w_ref[:, 3 * D_MODEL:4 * D_MODEL])
    sg_ref[0] = (gate / (1.0 + jnp.exp(-gate))).astype(BF16)


def _in_proj(x, norm_g, w_bf16, qk_gains, tm, v_blocks, q_scale):
    b, s, _ = x.shape
    row = lambda bi, si: (bi, si, 0)
    fixed = lambda bi, si: (0, 0)
    in_specs = [pl.BlockSpec((1, tm, D_MODEL), row),
                pl.BlockSpec((1, D_MODEL), fixed),
                pl.BlockSpec((D_MODEL, 4 * D_MODEL), fixed)]
    args = [x, norm_g.reshape(1, D_MODEL), w_bf16]
    if qk_gains is not None:
        q_gain, k_gain = qk_gains
        r = jnp.arange(256)
        bd = jnp.where((r[:, None] // DH) == (r[None, :] // DH), 1.0 / DH, 0.0).astype(BF16)
        in_specs += [pl.BlockSpec((1, D_MODEL), fixed), pl.BlockSpec((1, D_MODEL), fixed),
                     pl.BlockSpec((256, 256), fixed)]
        args += [jnp.tile(q_gain.astype(F32), D_MODEL // DH).reshape(1, D_MODEL),
                 jnp.tile(k_gain.astype(F32), D_MODEL // DH).reshape(1, D_MODEL), bd]
    out_block = pl.BlockSpec((1, tm, D_MODEL), row)
    shp = lambda dt: jax.ShapeDtypeStruct((b, s, D_MODEL), dt)
    if v_blocks:
        vb_block = pl.BlockSpec((1, N_GROUPS, 1, HEAD_COLS, tm), lambda bi, si: (bi, 0, si, 0, 0))
        vb_shape = jax.ShapeDtypeStruct((b, N_GROUPS, s // tm, HEAD_COLS, tm), BF16)
    else:
        vb_block, vb_shape = out_block, shp(BF16)
    return pl.pallas_call(
        functools.partial(_inproj_kernel, qk_norm=qk_gains is not None, v_blocks=v_blocks, q_scale=q_scale),
        grid=(b, s // tm),
        in_specs=in_specs,
        out_specs=[out_block] * 4 + [vb_block, out_block],
        out_shape=[shp(BF16), shp(F32), shp(F32), shp(BF16), vb_shape, shp(BF16)],
        compiler_params=_cparams(2),
        name="in_proj_qknorm" if qk_gains is not None else "in_proj",
    )(*args)


def _outproj_kernel(x_ref, a_ref, w_ref, y_ref):
    y_ref[...] = x_ref[...] + _dot(a_ref[...], w_ref[...])


def _out_proj(x2d, a_bf16, w_bf16, tm):
    n = x2d.shape[0]
    row = lambda i: (i, 0)
    return pl.pallas_call(
        _outproj_kernel,
        grid=(n // tm,),
        in_specs=[pl.BlockSpec((tm, D_MODEL), row), pl.BlockSpec((tm, D_MODEL), row),
                  pl.BlockSpec((D_MODEL, D_MODEL), lambda i: (0, 0))],
        out_specs=pl.BlockSpec((tm, D_MODEL), row),
        out_shape=jax.ShapeDtypeStruct((n, D_MODEL), F32),
        compiler_params=_cparams(1),
        name="out_proj",
    )(x2d, a_bf16, w_bf16)


def _split_halves(q):
    lane = lax.broadcasted_iota(jnp.int32, q.shape, 1)
    zero = jnp.zeros_like(q)
    return jnp.concatenate([jnp.where(lane < DH, q, zero), jnp.where(lane >= DH, q, zero)], axis=0)


def _diff_lambda(lam_ref, lam_init):
    t1 = jnp.sum(lam_ref[0:1, :] * lam_ref[1:2, :], axis=-1, keepdims=True)
    t2 = jnp.sum(lam_ref[2:3, :] * lam_ref[3:4, :], axis=-1, keepdims=True)
    return jnp.exp(t1) - jnp.exp(t2) + lam_init


def _sb_terms(z2):
    lp = jnp.log2(1.0 + jnp.exp2(-jnp.abs(z2)))
    log_beta = jnp.minimum(z2, 0.0) - lp
    return log_beta, log_beta - z2


def _split_bf16(x):
    hi = x.astype(BF16)
    return hi, (x - hi.astype(F32)).astype(BF16)


def _diff_prompt_kernel(slope_ref, lam_ref, subg_ref, boff_ref, bdiag_ref,
                        q_ref, k_ref, vt_ref, sg_ref, o_ref,
                        m_ref, l_ref, acc_ref, s_ref, *, lam_init):
    heads = range(DIFF_GROUPS)
    cols = lambda g: slice(g * HEAD_COLS, (g + 1) * HEAD_COLS)
    slopes = [slope_ref[pl.program_id(1) * DIFF_GROUPS + g] for g in heads]
    lam = _diff_lambda(lam_ref, lam_init)

    def q_block(qi, carry):
        q0 = pl.multiple_of(qi * TQ, TQ)
        q = q_ref[0, pl.ds(q0, TQ), :]
        qq = [_split_halves(q[:, cols(g)]) for g in heads]
        m_ref[...] = jnp.full(m_ref.shape, NEG_BIG, F32)
        l_ref[...] = jnp.zeros(l_ref.shape, F32)
        acc_ref[...] = jnp.zeros(acc_ref.shape, F32)

        def scores(kj):
            k = k_ref[0, pl.ds(pl.multiple_of(kj * TK, TK), TK), :]
            return [_dot_nt(k[:, cols(g)], qq[g]) for g in heads]

        def block(kj, bias_ref, block_dist, has_next):
            s_cur = [s_ref[g] for g in heads]
            if has_next:
                s_next = scores(kj + 1)
            p, alpha = [], []
            for g in heads:
                bias = bias_ref[g]
                s = jnp.concatenate([s_cur[g][:, 0:TQ] - bias, s_cur[g][:, TQ:2 * TQ] - bias], axis=1)
                shift = slopes[g] * block_dist
                m_prev = m_ref[g]
                m_new = jnp.maximum(m_prev, jnp.max(s, axis=0, keepdims=True) - shift)
                pg = jnp.exp2(s - (m_new + shift))
                ag = jnp.exp2(m_prev - m_new)
                l_ref[g] = ag * l_ref[g] + jnp.sum(pg, axis=0, keepdims=True)
                m_ref[g] = m_new
                p.append(pg.astype(BF16))
                alpha.append(ag)
            for g in heads:
                acc_ref[g] = alpha[g] * acc_ref[g] + _dot(vt_ref[0, g, kj], p[g])
            if has_next:
                for g in heads:
                    s_ref[g] = s_next[g]

        first = scores(0)
        for g in heads:
            s_ref[g] = first[g]

        def past(kj, c):
            block(kj, boff_ref, ((qi - kj) * TQ).astype(F32), True)
            return c

        lax.fori_loop(0, qi, past, 0)
        block(qi, bdiag_ref, 0.0, False)

        outs = []
        for g in heads:
            inv_l = 1.0 / l_ref[g]
            o_t = (acc_ref[g, :, 0:TQ] * inv_l[:, 0:TQ]
                   - lam * (acc_ref[g, :, TQ:2 * TQ] * inv_l[:, TQ:2 * TQ]))
            ms = jnp.mean(o_t * o_t, axis=0, keepdims=True)
            outs.append((o_t * lax.rsqrt(ms + RMS_EPS)).T * subg_ref[...] * (1.0 - lam_init))
        on = jnp.concatenate(outs, axis=1)
        o_ref[0, pl.ds(q0, TQ), :] = (on * sg_ref[0, pl.ds(q0, TQ), :].astype(F32)).astype(BF16)
        return carry

    lax.fori_loop(0, q_ref.shape[1] // TQ, q_block, 0)


def _sb_prompt_kernel(u_ref, q_ref, k_ref, v_ref, sg_ref, o_ref, c_ref, acc_ref, z_ref):
    lane = lax.broadcasted_iota(jnp.int32, (TQ, HEAD_COLS), 1)
    n_heads = 2 * SB_GROUPS
    cols = lambda hd: slice((hd // 2) * HEAD_COLS, (hd // 2 + 1) * HEAD_COLS)

    def q_block(qi, carry):
        q0 = pl.multiple_of(qi * TQ, TQ)
        q = q_ref[0, pl.ds(q0, TQ), :]
        zero = jnp.zeros((TQ, HEAD_COLS), BF16)
        q_heads = [jnp.where((lane >= DH) == bool(hd % 2), q[:, cols(hd)], zero) for hd in range(n_heads)]
        c_ref[...] = jnp.zeros(c_ref.shape, F32)
        acc_ref[...] = jnp.zeros(acc_ref.shape, F32)

        def scores(kj):
            k = k_ref[0, pl.ds(pl.multiple_of(kj * TK, TK), TK), :]
            return [_dot_nt(q_heads[hd], k[:, cols(hd)]) for hd in range(n_heads)]

        def block(kj, diagonal):
            z = [z_ref[hd] for hd in range(n_heads)]
            z_next = scores(jnp.maximum(kj - 1, 0))
            v = v_ref[0, pl.ds(pl.multiple_of(kj * TK, TK), TK), :]
            if diagonal:
                qry = lax.broadcasted_iota(jnp.int32, (TQ, TK), 0)
                key = lax.broadcasted_iota(jnp.int32, (TQ, TK), 1)
                before = key < qry
            log_beta, later = [], []
            for hd in range(n_heads):
                lb, log_keep = _sb_terms(z[hd])
                if diagonal:
                    log_keep = jnp.where(before, log_keep, 0.0)
                log_beta.append(lb)
                later.append(_dot(log_keep.astype(BF16), u_ref[...]) + c_ref[hd])
                c_ref[hd] += jnp.sum(log_keep, axis=1, keepdims=True)
            for hd in range(n_heads):
                a = jnp.exp2(log_beta[hd] + later[hd])
                if diagonal:
                    a = jnp.where(before, a, 0.0)
                acc_ref[hd] += _dot(a.astype(BF16), v[:, cols(hd)])
            for hd in range(n_heads):
                z_ref[hd] = z_next[hd]

        first = scores(qi)
        for hd in range(n_heads):
            z_ref[hd] = first[hd]
        block(qi, True)

        def earlier(i, c):
            block(qi - 1 - i, False)
            return c

        lax.fori_loop(0, qi, earlier, 0)

        o = jnp.concatenate([jnp.where(lane < DH, acc_ref[2 * g], acc_ref[2 * g + 1])
                             for g in range(SB_GROUPS)], axis=1)
        o_ref[0, pl.ds(q0, TQ), :] = (o * sg_ref[0, pl.ds(q0, TQ), :].astype(F32)).astype(BF16)
        return carry

    lax.fori_loop(0, q_ref.shape[1] // TQ, q_block, 0)


def _later_key_matrix(n):
    r = jnp.arange(n)
    return (r[:, None] > r[None, :]).astype(BF16)


def _alibi_slopes_log2():
    return LOG2_E * 2.0 ** (-8.0 * jnp.arange(1, H_A + 1, dtype=F32) / H_A)


def _diff_prompt_attention(q, kb, vt, sg, lam_rows, subln_g, lam_init):
    b, s, _ = q.shape
    slopes = _alibi_slopes_log2()
    key = jnp.arange(TK)[:, None]
    qry = jnp.arange(TQ)[None, :]
    rel = (qry - key).astype(F32)
    visible = (key // CHUNK) <= (qry // CHUNK)
    bias_off = slopes[:, None, None] * rel[None]
    bias_diag = jnp.where(visible[None], slopes[:, None, None] * jnp.abs(rel)[None], -NEG_BIG)
    grp = lambda bi, h: (bi, 0, h)
    seq = pl.BlockSpec((1, s, DIFF_GROUPS * HEAD_COLS), grp)
    vtspec = pl.BlockSpec((1, DIFF_GROUPS, s // TK, HEAD_COLS, TK), lambda bi, h: (bi, h, 0, 0, 0))
    biasblk = pl.BlockSpec((DIFF_GROUPS, TK, TQ), lambda bi, h: (h, 0, 0))
    fixed2 = lambda bi, h: (0, 0)
    return pl.pallas_call(
        functools.partial(_diff_prompt_kernel, lam_init=lam_init),
        grid=(b, N_GROUPS // DIFF_GROUPS),
        in_specs=[pl.BlockSpec(memory_space=pltpu.SMEM),
                  pl.BlockSpec((8, HEAD_COLS), fixed2), pl.BlockSpec((1, HEAD_COLS), fixed2),
                  biasblk, biasblk, seq, seq, vtspec, seq],
        out_specs=seq,
        out_shape=jax.ShapeDtypeStruct((b, s, D_MODEL), BF16),
        scratch_shapes=[pltpu.VMEM((DIFF_GROUPS, 1, 2 * TQ), F32), pltpu.VMEM((DIFF_GROUPS, 1, 2 * TQ), F32),
                        pltpu.VMEM((DIFF_GROUPS, HEAD_COLS, 2 * TQ), F32),
                        pltpu.VMEM((DIFF_GROUPS, TK, 2 * TQ), F32)],
        compiler_params=_cparams(2),
        name="diff_attn_prompt",
    )(slopes, lam_rows, subln_g.reshape(1, HEAD_COLS).astype(F32), bias_off, bias_diag, q, kb, vt, sg)


def _sb_prompt_attention(q, kb, vb, sg):
    b, s, _ = q.shape
    seq = pl.BlockSpec((1, s, SB_GROUPS * HEAD_COLS), lambda bi, h: (bi, 0, h))
    n_heads = 2 * SB_GROUPS
    return pl.pallas_call(
        _sb_prompt_kernel,
        grid=(b, N_GROUPS // SB_GROUPS),
        in_specs=[pl.BlockSpec((TK, TK), lambda bi, h: (0, 0)), seq, seq, seq, seq],
        out_specs=seq,
        out_shape=jax.ShapeDtypeStruct((b, s, D_MODEL), BF16),
        scratch_shapes=[pltpu.VMEM((n_heads, TQ, 1), F32), pltpu.VMEM((n_heads, TQ, HEAD_COLS), F32),
                        pltpu.VMEM((n_heads, TQ, TK), F32)],
        compiler_params=_cparams(2),
        name="sb_attn_prompt",
    )(_later_key_matrix(TK), q, kb, vb, sg)


def _softmax_step(s, v, m_ref, l_ref, acc_ref):
    m_prev = m_ref[...]
    m_new = jnp.maximum(m_prev, jnp.max(s, axis=1, keepdims=True))
    p = jnp.exp2(s - m_new)
    alpha = jnp.exp2(m_prev - m_new)
    l_ref[...] = alpha * l_ref[...] + jnp.sum(p, axis=1, keepdims=True)
    acc_ref[...] = alpha * acc_ref[...] + _dot(p.astype(BF16), v)
    m_ref[...] = m_new


def _diff_decode_kernel(slope_ref, lam_ref, subg_ref, q_ref, ck_ref, cv_ref, nk_ref, nv_ref, sg_ref,
                        o_ref, m_ref, l_ref, acc_ref, *, lam_init, past_len, n_new):
    slope = slope_ref[pl.program_id(1)]
    t = q_ref.shape[1]
    qq = _split_halves(q_ref[0])
    m_ref[...] = jnp.full(m_ref.shape, NEG_BIG, F32)
    l_ref[...] = jnp.zeros(l_ref.shape, F32)
    acc_ref[...] = jnp.zeros(acc_ref.shape, F32)

    def positions(n_keys, key0):
        q_pos = past_len + (lax.broadcasted_iota(jnp.int32, (2 * t, n_keys), 0) & (t - 1))
        k_pos = key0 + lax.broadcasted_iota(jnp.int32, (2 * t, n_keys), 1)
        return q_pos, k_pos

    def scores(k, q_pos, k_pos, valid):
        dist = jnp.abs(q_pos - k_pos).astype(F32)
        visible = valid & ((k_pos // CHUNK) <= (q_pos // CHUNK))
        return jnp.where(visible, _dot_nt(qq, k) - slope * dist, NEG_BIG)

    q_pos, k_pos = positions(past_len, 0)
    s = scores(ck_ref[0].astype(BF16), q_pos, k_pos, k_pos < past_len)
    _softmax_step(s, cv_ref[0].astype(BF16), m_ref, l_ref, acc_ref)

    q_pos, k_pos = positions(NEW_PAD, past_len)
    s = scores(nk_ref[0], q_pos, k_pos, k_pos < past_len + n_new)
    _softmax_step(s, nv_ref[0], m_ref, l_ref, acc_ref)

    lam = _diff_lambda(lam_ref, lam_init)
    inv_l = 1.0 / l_ref[...]
    o = acc_ref[0:t, :] * inv_l[0:t] - lam * (acc_ref[t:2 * t, :] * inv_l[t:2 * t])
    ms = jnp.mean(o * o, axis=-1, keepdims=True)
    on = o * lax.rsqrt(ms + RMS_EPS) * subg_ref[...] * (1.0 - lam_init)
    o_ref[0] = (on * sg_ref[0].astype(F32)).astype(BF16)


def _sb_decode_kernel(u_ref, q_ref, ck_ref, cv_ref, nk_ref, nv_ref, sg_ref, o_ref, acc_ref,
                      *, past_len, n_new):
    t = q_ref.shape[1]
    qq = _split_halves(q_ref[0])

    def later_sums(log_keep, u):
        hi, lo = _split_bf16(log_keep)
        return _dot(hi, u) + _dot(lo, u)

    row = lax.broadcasted_iota(jnp.int32, (2 * t, NEW_PAD), 0) & (t - 1)
    col = lax.broadcasted_iota(jnp.int32, (2 * t, NEW_PAD), 1)
    before = (col < row) & (col < n_new)
    log_beta, log_keep = _sb_terms(_dot_nt(qq, nk_ref[0]))
    log_keep = jnp.where(before, log_keep, 0.0)
    a = jnp.exp2(log_beta + later_sums(log_keep, u_ref[0:NEW_PAD, 0:NEW_PAD]))
    a = jnp.where(before, a, 0.0)
    acc_ref[...] = _dot(a.astype(BF16), nv_ref[0])
    carry = jnp.sum(log_keep, axis=1, keepdims=True)

    for blk in range(past_len // TK - 1, -1, -1):
        k = ck_ref[0, blk * TK:(blk + 1) * TK, :].astype(BF16)
        v = cv_ref[0, blk * TK:(blk + 1) * TK, :].astype(BF16)
        log_beta, log_keep = _sb_terms(_dot_nt(qq, k))
        a = jnp.exp2(log_beta + later_sums(log_keep, u_ref[...]) + carry)
        acc_ref[...] += _dot(a.astype(BF16), v)
        carry = carry + jnp.sum(log_keep, axis=1, keepdims=True)

    lane = lax.broadcasted_iota(jnp.int32, (t, HEAD_COLS), 1)
    o = jnp.where(lane < DH, acc_ref[0:t, :], acc_ref[t:2 * t, :])
    o_ref[0] = (o * sg_ref[0].astype(F32)).astype(BF16)


def _pad_new(x):
    return jnp.pad(x, ((0, 0), (0, NEW_PAD - x.shape[1]), (0, 0)))


def _decode_specs(t, past_len):
    grp = lambda bi, h: (bi, 0, h)
    qblk = pl.BlockSpec((1, t, HEAD_COLS), grp)
    cblk = pl.BlockSpec((1, past_len, HEAD_COLS), grp)
    nblk = pl.BlockSpec((1, NEW_PAD, HEAD_COLS), grp)
    return qblk, cblk, nblk


def _diff_decode_attention(q, cache_k, cache_v, kb_new, vb_new, sg, lam_rows, subln_g, lam_init):
    b, t, _ = q.shape
    past_len = cache_k.shape[1]
    qblk, cblk, nblk = _decode_specs(t, past_len)
    fixed2 = lambda bi, h: (0, 0)
    return pl.pallas_call(
        functools.partial(_diff_decode_kernel, lam_init=lam_init, past_len=past_len, n_new=t),
        grid=(b, N_GROUPS),
        in_specs=[pl.BlockSpec(memory_space=pltpu.SMEM),
                  pl.BlockSpec((8, HEAD_COLS), fixed2), pl.BlockSpec((1, HEAD_COLS), fixed2),
                  qblk, cblk, cblk, nblk, nblk, qblk],
        out_specs=qblk,
        out_shape=jax.ShapeDtypeStruct((b, t, D_MODEL), BF16),
        scratch_shapes=[pltpu.VMEM((2 * t, 1), F32), pltpu.VMEM((2 * t, 1), F32),
                        pltpu.VMEM((2 * t, HEAD_COLS), F32)],
        compiler_params=_cparams(2),
        name="diff_attn_decode",
    )(_alibi_slopes_log2(), lam_rows, subln_g.reshape(1, HEAD_COLS).astype(F32),
      q, cache_k, cache_v, _pad_new(kb_new), _pad_new(vb_new), sg)


def _sb_decode_attention(q, cache_k, cache_v, kb_new, vb_new, sg):
    b, t, _ = q.shape
    past_len = cache_k.shape[1]
    qblk, cblk, nblk = _decode_specs(t, past_len)
    return pl.pallas_call(
        functools.partial(_sb_decode_kernel, past_len=past_len, n_new=t),
        grid=(b, N_GROUPS),
        in_specs=[pl.BlockSpec((TK, TK), lambda bi, h: (0, 0)), qblk, cblk, cblk, nblk, nblk, qblk],
        out_specs=qblk,
        out_shape=jax.ShapeDtypeStruct((b, t, D_MODEL), BF16),
        scratch_shapes=[pltpu.VMEM((2 * t, HEAD_COLS), F32)],
        compiler_params=_cparams(2),
        name="sb_attn_decode",
    )(_later_key_matrix(TK), q, cache_k, cache_v, _pad_new(kb_new), _pad_new(vb_new), sg)


def _prompt_layer(x, proj, w_out, attend, v_blocks):
    b, s, _ = x.shape
    q, k, v, kb, vb, sg = _in_proj(x, *proj[:3], tm=TK, v_blocks=v_blocks, q_scale=proj[3])
    a = attend(q, kb, vb, sg)
    y = _out_proj(x.reshape(b * s, D_MODEL), a.reshape(b * s, D_MODEL), w_out, tm=512)
    return y.reshape(b, s, D_MODEL), k, v


def _decode_layer(x, cache_k, cache_v, proj, w_out, attend):
    b, t, _ = x.shape
    n = b * t
    past = cache_k.shape[1]
    outs = _in_proj(x.reshape(1, n, D_MODEL), *proj[:3], tm=n, v_blocks=False, q_scale=proj[3])
    q, k, v, kb, vb, sg = [o.reshape(b, t, D_MODEL) for o in outs]
    a = attend(q, cache_k.reshape(b, past, D_MODEL), cache_v.reshape(b, past, D_MODEL), kb, vb, sg)
    y = _out_proj(x.reshape(n, D_MODEL), a.reshape(n, D_MODEL), w_out, tm=n)
    return y.reshape(b, t, D_MODEL), k, v


def kernel(x_prompt, x_sample, cache_k_0, cache_v_0, cache_k_1, cache_v_1, norm_g_0, w_in_0, q_norm_0, k_norm_0, lambda_q1_0, lambda_k1_0, lambda_q2_0, lambda_k2_0, subln_g_0, w_out_0, norm_g_1, w_in_1, w_out_1):
    bp, sp, _ = x_prompt.shape
    bs, ss, _ = x_sample.shape

    lam_init = 0.8 - 0.6 * math.exp(-0.3 * 0)
    lam_rows = jnp.zeros((8, HEAD_COLS), F32)
    for r, vec in enumerate((lambda_q1_0, lambda_k1_0, lambda_q2_0, lambda_k2_0)):
        lam_rows = lam_rows.at[r, 0:DH].set(vec.astype(F32))

    diff_prompt = functools.partial(_diff_prompt_attention, lam_rows=lam_rows, subln_g=subln_g_0,
                                    lam_init=lam_init)
    diff_decode = functools.partial(_diff_decode_attention, lam_rows=lam_rows, subln_g=subln_g_0,
                                    lam_init=lam_init)
    w_out_0b, w_out_1b = w_out_0.astype(BF16), w_out_1.astype(BF16)
    proj0 = (norm_g_0, w_in_0.astype(BF16), (q_norm_0, k_norm_0), QK_SCALE * LOG2_E)
    proj1 = (norm_g_1, w_in_1.astype(BF16), None, QK_SCALE * LOG2_E)

    yp, k0p, v0p = _prompt_layer(x_prompt, proj0, w_out_0b, diff_prompt, v_blocks=True)
    ys, k0s, v0s = _decode_layer(x_sample, cache_k_0, cache_v_0, proj0, w_out_0b, diff_decode)
    yp, k1p, v1p = _prompt_layer(yp, proj1, w_out_1b, _sb_prompt_attention, v_blocks=False)
    ys, k1s, v1s = _decode_layer(ys, cache_k_1, cache_v_1, proj1, w_out_1b, _sb_decode_attention)

    return (yp, ys,
            k0p.reshape(bp, sp, H_A, 2 * DH), v0p.reshape(bp, sp, H_A, 2 * DH),
            k0s.reshape(bs, ss, H_A, 2 * DH), v0s.reshape(bs, ss, H_A, 2 * DH),
            k1p.reshape(bp, sp, 2 * H_A, DH), v1p.reshape(bp, sp, 2 * H_A, DH),
            k1s.reshape(bs, ss, 2 * H_A, DH), v1s.reshape(bs, ss, 2 * H_A, DH))
```

```python
import functools
import math

import jax
import jax.numpy as jnp
from jax import lax
from jax.experimental import pallas as pl
from jax.experimental.pallas import tpu as pltpu

F32 = jnp.float32
BF16 = jnp.bfloat16

D_MODEL = 1024
CHUNK = 64
H_A = 8
DH = 64
HEAD_COLS = 128
N_GROUPS = D_MODEL // HEAD_COLS
RMS_EPS = 1e-6
NEG_BIG = -1e30
QK_SCALE = DH ** -0.5
LOG2_E = math.log2(math.e)

TQ = 256
TK = 256
NEW_PAD = 128
SB_GROUPS = 2
DIFF_GROUPS = 4
DEC_CHUNK = 256

VMEM_LIMIT = 48 * 1024 * 1024


def _cparams(n_axes):
    return pltpu.CompilerParams(
        dimension_semantics=("arbitrary",) * n_axes, vmem_limit_bytes=VMEM_LIMIT)


def _dot(a, b):
    return jnp.dot(a, b, preferred_element_type=F32)


def _dot_nt(a, b):
    return lax.dot_general(a, b, (((1,), (1,)), ((), ())), preferred_element_type=F32)


def _group_rms(t, gain_ref, bd_ref):
    sq = (t * t).astype(BF16)
    outs = []
    for c in range(0, D_MODEL, 256):
        ms = _dot(sq[:, c:c + 256], bd_ref[...])
        outs.append(t[:, c:c + 256] * lax.rsqrt(ms + RMS_EPS) * gain_ref[:, c:c + 256])
    return jnp.concatenate(outs, axis=1)


def _inproj_kernel(*refs, qk_norm, v_blocks, q_scale):
    if qk_norm:
        (x_ref, g_ref, w_ref, qg_ref, kg_ref, bd_ref,
         q_ref, k_ref, v_ref, kb_ref, vb_ref, sg_ref) = refs
    else:
        x_ref, g_ref, w_ref, q_ref, k_ref, v_ref, kb_ref, vb_ref, sg_ref = refs
    x = x_ref[0]
    ms = jnp.mean(x * x, axis=-1, keepdims=True)
    h = (x * lax.rsqrt(ms + RMS_EPS) * g_ref[...]).astype(BF16)

    q = _dot(h, w_ref[:, 0:D_MODEL])
    if qk_norm:
        q = _group_rms(q, qg_ref, bd_ref)
    q_ref[0] = (q * q_scale).astype(BF16)

    k = _dot(h, w_ref[:, D_MODEL:2 * D_MODEL])
    if qk_norm:
        k = _group_rms(k, kg_ref, bd_ref)
    k_ref[0] = k
    kb_ref[0] = k.astype(BF16)

    v = _dot(h, w_ref[:, 2 * D_MODEL:3 * D_MODEL])
    v_ref[0] = v
    if v_blocks:
        for g in range(N_GROUPS):
            vb_ref[0, g, 0] = v[:, g * HEAD_COLS:(g + 1) * HEAD_COLS].T.astype(BF16)
    else:
        vb_ref[0] = v.astype(BF16)

    gate = _dot(h, w_ref[:, 3 * D_MODEL:4 * D_MODEL])
    sg_ref[0] = (gate / (1.0 + jnp.exp(-gate))).astype(BF16)


def _in_proj(x, norm_g, w_bf16, qk_gains, tm, v_blocks, q_scale):
    b, s, _ = x.shape
    row = lambda bi, si: (bi, si, 0)
    fixed = lambda bi, si: (0, 0)
    in_specs = [pl.BlockSpec((1, tm, D_MODEL), row),
                pl.BlockSpec((1, D_MODEL), fixed),
                pl.BlockSpec((D_MODEL, 4 * D_MODEL), fixed)]
    args = [x, norm_g.reshape(1, D_MODEL), w_bf16]
    if qk_gains is not None:
        q_gain, k_gain = qk_gains
        r = jnp.arange(256)
        bd = jnp.where((r[:, None] // DH) == (r[None, :] // DH), 1.0 / DH, 0.0).astype(BF16)
        in_specs += [pl.BlockSpec((1, D_MODEL), fixed), pl.BlockSpec((1, D_MODEL), fixed),
                     pl.BlockSpec((256, 256), fixed)]
        args += [jnp.tile(q_gain.astype(F32), D_MODEL // DH).reshape(1, D_MODEL),
                 jnp.tile(k_gain.astype(F32), D_MODEL // DH).reshape(1, D_MODEL), bd]
    out_block = pl.BlockSpec((1, tm, D_MODEL), row)
    shp = lambda dt: jax.ShapeDtypeStruct((b, s, D_MODEL), dt)
    if v_blocks:
        vb_block = pl.BlockSpec((1, N_GROUPS, 1, HEAD_COLS, tm), lambda bi, si: (bi, 0, si, 0, 0))
        vb_shape = jax.ShapeDtypeStruct((b, N_GROUPS, s // tm, HEAD_COLS, tm), BF16)
    else:
        vb_block, vb_shape = out_block, shp(BF16)
    return pl.pallas_call(
        functools.partial(_inproj_kernel, qk_norm=qk_gains is not None, v_blocks=v_blocks, q_scale=q_scale),
        grid=(b, s // tm),
        in_specs=in_specs,
        out_specs=[out_block] * 4 + [vb_block, out_block],
        out_shape=[shp(BF16), shp(F32), shp(F32), shp(BF16), vb_shape, shp(BF16)],
        compiler_params=_cparams(2),
        name="in_proj_qknorm" if qk_gains is not None else "in_proj",
    )(*args)


def _outproj_kernel(x_ref, a_ref, w_ref, y_ref):
    y_ref[...] = x_ref[...] + _dot(a_ref[...], w_ref[...])


def _out_proj(x2d, a_bf16, w_bf16, tm):
    n = x2d.shape[0]
    row = lambda i: (i, 0)
    return pl.pallas_call(
        _outproj_kernel,
        grid=(n // tm,),
        in_specs=[pl.BlockSpec((tm, D_MODEL), row), pl.BlockSpec((tm, D_MODEL), row),
                  pl.BlockSpec((D_MODEL, D_MODEL), lambda i: (0, 0))],
        out_specs=pl.BlockSpec((tm, D_MODEL), row),
        out_shape=jax.ShapeDtypeStruct((n, D_MODEL), F32),
        compiler_params=_cparams(1),
        name="out_proj",
    )(x2d, a_bf16, w_bf16)


def _split_halves(q):
    lane = lax.broadcasted_iota(jnp.int32, q.shape, 1)
    zero = jnp.zeros_like(q)
    return jnp.concatenate([jnp.where(lane < DH, q, zero), jnp.where(lane >= DH, q, zero)], axis=0)


def _diff_lambda(lam_ref, lam_init):
    t1 = jnp.sum(lam_ref[0:1, :] * lam_ref[1:2, :], axis=-1, keepdims=True)
    t2 = jnp.sum(lam_ref[2:3, :] * lam_ref[3:4, :], axis=-1, keepdims=True)
    return jnp.exp(t1) - jnp.exp(t2) + lam_init


def _sb_terms(z2):
    lp = jnp.log2(1.0 + jnp.exp2(-jnp.abs(z2)))
    log_beta = jnp.minimum(z2, 0.0) - lp
    return log_beta, log_beta - z2


def _split_bf16(x):
    hi = x.astype(BF16)
    return hi, (x - hi.astype(F32)).astype(BF16)


def _diff_prompt_kernel(slope_ref, lam_ref, subg_ref, boff_ref, bdiag_ref,
                        q_ref, k_ref, vt_ref, sg_ref, o_ref,
                        m_ref, l_ref, acc_ref, *, lam_init):
    heads = range(DIFF_GROUPS)
    cols = lambda g: slice(g * HEAD_COLS, (g + 1) * HEAD_COLS)
    slopes = [slope_ref[pl.program_id(1) * DIFF_GROUPS + g] for g in heads]
    lam = _diff_lambda(lam_ref, lam_init)

    def q_block(qi, carry):
        q0 = pl.multiple_of(qi * TQ, TQ)
        q = q_ref[0, pl.ds(q0, TQ), :]
        qq = [_split_halves(q[:, cols(g)]) for g in heads]
        m_ref[...] = jnp.full(m_ref.shape, NEG_BIG, F32)
        l_ref[...] = jnp.zeros(l_ref.shape, F32)
        acc_ref[...] = jnp.zeros(acc_ref.shape, F32)

        def block(kj, bias_ref, block_dist):
            k = k_ref[0, pl.ds(pl.multiple_of(kj * TK, TK), TK), :]
            scores = [_dot_nt(k[:, cols(g)], qq[g]) for g in heads]
            p, alpha = [], []
            for g in heads:
                bias = bias_ref[g]
                s = jnp.concatenate([scores[g][:, 0:TQ] - bias, scores[g][:, TQ:2 * TQ] - bias], axis=1)
                shift = slopes[g] * block_dist
                m_prev = m_ref[g]
                m_new = jnp.maximum(m_prev, jnp.max(s, axis=0, keepdims=True) - shift)
                pg = jnp.exp2(s - (m_new + shift))
                ag = jnp.exp2(m_prev - m_new)
                l_ref[g] = ag * l_ref[g] + jnp.sum(pg, axis=0, keepdims=True)
                m_ref[g] = m_new
                p.append(pg.astype(BF16))
                alpha.append(ag)
            for g in heads:
                acc_ref[g] = alpha[g] * acc_ref[g] + _dot(vt_ref[0, g, kj], p[g])

        def past(kj, c):
            block(kj, boff_ref, (qi - kj) * TQ)
            return c

        lax.fori_loop(0, qi, past, 0)
        block(qi, bdiag_ref, 0.0)

        outs = []
        for g in heads:
            inv_l = 1.0 / l_ref[g]
            o_t = (acc_ref[g, :, 0:TQ] * inv_l[:, 0:TQ]
                   - lam * (acc_ref[g, :, TQ:2 * TQ] * inv_l[:, TQ:2 * TQ]))
            ms = jnp.mean(o_t * o_t, axis=0, keepdims=True)
            outs.append((o_t * lax.rsqrt(ms + RMS_EPS)).T * subg_ref[...] * (1.0 - lam_init))
        on = jnp.concatenate(outs, axis=1)
        o_ref[0, pl.ds(q0, TQ), :] = (on * sg_ref[0, pl.ds(q0, TQ), :].astype(F32)).astype(BF16)
        return carry

    lax.fori_loop(0, q_ref.shape[1] // TQ, q_block, 0)


def _sb_prompt_kernel(u_ref, q_ref, k_ref, v_ref, sg_ref, o_ref, c_ref, acc_ref, z_ref):
    lane = lax.broadcasted_iota(jnp.int32, (TQ, HEAD_COLS), 1)
    n_heads = 2 * SB_GROUPS
    cols = lambda hd: slice((hd // 2) * HEAD_COLS, (hd // 2 + 1) * HEAD_COLS)

    def q_block(qi, carry):
        q0 = pl.multiple_of(qi * TQ, TQ)
        q = q_ref[0, pl.ds(q0, TQ), :]
        zero = jnp.zeros((TQ, HEAD_COLS), BF16)
        q_heads = [jnp.where((lane >= DH) == bool(hd % 2), q[:, cols(hd)], zero) for hd in range(n_heads)]
        c_ref[...] = jnp.zeros(c_ref.shape, F32)
        acc_ref[...] = jnp.zeros(acc_ref.shape, F32)

        def scores(kj):
            k = k_ref[0, pl.ds(pl.multiple_of(kj * TK, TK), TK), :]
            return [_dot_nt(q_heads[hd], k[:, cols(hd)]) for hd in range(n_heads)]

        def block(kj, diagonal):
            z = [z_ref[hd] for hd in range(n_heads)]
            z_next = scores(jnp.maximum(kj - 1, 0))
            v = v_ref[0, pl.ds(pl.multiple_of(kj * TK, TK), TK), :]
            if diagonal:
                qry = lax.broadcasted_iota(jnp.int32, (TQ, TK), 0)
                key = lax.broadcasted_iota(jnp.int32, (TQ, TK), 1)
                before = key < qry
            log_beta, later = [], []
            for hd in range(n_heads):
                lb, log_keep = _sb_terms(z[hd])
                if diagonal:
                    log_keep = jnp.where(before, log_keep, 0.0)
                log_beta.append(lb)
                later.append(_dot(log_keep.astype(BF16), u_ref[...]) + c_ref[hd])
                c_ref[hd] += jnp.sum(log_keep, axis=1, keepdims=True)
            for hd in range(n_heads):
                a = jnp.exp2(log_beta[hd] + later[hd])
                if diagonal:
                    a = jnp.where(before, a, 0.0)
                acc_ref[hd] += _dot(a.astype(BF16), v[:, cols(hd)])
            for hd in range(n_heads):
                z_ref[hd] = z_next[hd]

        first = scores(qi)
        for hd in range(n_heads):
            z_ref[hd] = first[hd]
        block(qi, True)

        def earlier(i, c):
            block(qi - 1 - i, False)
            return c

        lax.fori_loop(0, qi, earlier, 0)

        o = jnp.concatenate([jnp.where(lane < DH, acc_ref[2 * g], acc_ref[2 * g + 1])
                             for g in range(SB_GROUPS)], axis=1)
        o_ref[0, pl.ds(q0, TQ), :] = (o * sg_ref[0, pl.ds(q0, TQ), :].astype(F32)).astype(BF16)
        return carry

    lax.fori_loop(0, q_ref.shape[1] // TQ, q_block, 0)


def _later_key_matrix(n):
    r = jnp.arange(n)
    return (r[:, None] > r[None, :]).astype(BF16)


def _alibi_slopes_log2():
    return LOG2_E * 2.0 ** (-8.0 * jnp.arange(1, H_A + 1, dtype=F32) / H_A)


def _diff_prompt_attention(q, kb, vt, sg, lam_rows, subln_g, lam_init):
    b, s, _ = q.shape
    slopes = _alibi_slopes_log2()
    key = jnp.arange(TK)[:, None]
    qry = jnp.arange(TQ)[None, :]
    rel = (qry - key).astype(F32)
    visible = (key // CHUNK) <= (qry // CHUNK)
    bias_off = slopes[:, None, None] * rel[None]
    bias_diag = jnp.where(visible[None], slopes[:, None, None] * jnp.abs(rel)[None], -NEG_BIG)
    grp = lambda bi, h: (bi, 0, h)
    seq = pl.BlockSpec((1, s, DIFF_GROUPS * HEAD_COLS), grp)
    vtspec = pl.BlockSpec((1, DIFF_GROUPS, s // TK, HEAD_COLS, TK), lambda bi, h: (bi, h, 0, 0, 0))
    biasblk = pl.BlockSpec((DIFF_GROUPS, TK, TQ), lambda bi, h: (h, 0, 0))
    fixed2 = lambda bi, h: (0, 0)
    return pl.pallas_call(
        functools.partial(_diff_prompt_kernel, lam_init=lam_init),
        grid=(b, N_GROUPS // DIFF_GROUPS),
        in_specs=[pl.BlockSpec(memory_space=pltpu.SMEM),
                  pl.BlockSpec((8, HEAD_COLS), fixed2), pl.BlockSpec((1, HEAD_COLS), fixed2),
                  biasblk, biasblk, seq, seq, vtspec, seq],
        out_specs=seq,
        out_shape=jax.ShapeDtypeStruct((b, s, D_MODEL), BF16),
        scratch_shapes=[pltpu.VMEM((DIFF_GROUPS, 1, 2 * TQ), F32), pltpu.VMEM((DIFF_GROUPS, 1, 2 * TQ), F32),
                        pltpu.VMEM((DIFF_GROUPS, HEAD_COLS, 2 * TQ), F32)],
        compiler_params=_cparams(2),
        name="diff_attn_prompt",
    )(slopes, lam_rows, subln_g.reshape(1, HEAD_COLS).astype(F32), bias_off, bias_diag, q, kb, vt, sg)


def _sb_prompt_attention(q, kb, vb, sg):
    b, s, _ = q.shape
    seq = pl.BlockSpec((1, s, SB_GROUPS * HEAD_COLS), lambda bi, h: (bi, 0, h))
    n_heads = 2 * SB_GROUPS
    return pl.pallas_call(
        _sb_prompt_kernel,
        grid=(b, N_GROUPS // SB_GROUPS),
        in_specs=[pl.BlockSpec((TK, TK), lambda bi, h: (0, 0)), seq, seq, seq, seq],
        out_specs=seq,
        out_shape=jax.ShapeDtypeStruct((b, s, D_MODEL), BF16),
        scratch_shapes=[pltpu.VMEM((n_heads, TQ, 1), F32), pltpu.VMEM((n_heads, TQ, HEAD_COLS), F32),
                        pltpu.VMEM((n_heads, TQ, TK), F32)],
        compiler_params=_cparams(2),
        name="sb_attn_prompt",
    )(_later_key_matrix(TK), q, kb, vb, sg)


def _diff_decode_kernel(lam_ref, subg_ref, slope_ref, bias_past_ref, bias_new_ref,
                        q_ref, ck_ref, cv_ref, nk_ref, nv_ref, sg_ref, o_ref,
                        qq_ref, m_ref, l_ref, acc_ref, *, lam_init, past_len):
    c = pl.program_id(1)
    t = q_ref.shape[1]
    half = H_A * t
    heads = [slice(h * HEAD_COLS, (h + 1) * HEAD_COLS) for h in range(H_A)]

    @pl.when(c == 0)
    def _():
        q = q_ref[0]
        lane = lax.broadcasted_iota(jnp.int32, (t, HEAD_COLS), 1)
        zero = jnp.zeros((t, HEAD_COLS), BF16)
        first = [jnp.where(lane < DH, q[:, hs], zero) for hs in heads]
        second = [jnp.where(lane >= DH, q[:, hs], zero) for hs in heads]
        qq_ref[...] = jnp.concatenate(first + second, axis=0)
        m_ref[...] = jnp.full(m_ref.shape, NEG_BIG, F32)
        l_ref[...] = jnp.zeros(l_ref.shape, F32)
        acc_ref[...] = jnp.zeros(acc_ref.shape, F32)

    def update(k, v, bias, shift):
        s = _dot_nt(qq_ref[...], k) - bias
        m_prev = m_ref[...]
        m_new = jnp.maximum(m_prev, jnp.max(s, axis=1, keepdims=True) - shift)
        p = jnp.exp2(s - (m_new + shift))
        alpha = jnp.exp2(m_prev - m_new)
        l_ref[...] = alpha * l_ref[...] + jnp.sum(p, axis=1, keepdims=True)
        acc_ref[...] = alpha * acc_ref[...] + _dot(p.astype(BF16), v)
        m_ref[...] = m_new

    chunk_dist = past_len - c * DEC_CHUNK
    update(ck_ref[0].astype(BF16), cv_ref[0].astype(BF16), bias_past_ref[...], slope_ref[...] * chunk_dist)

    @pl.when(c == pl.num_programs(1) - 1)
    def _():
        nk, nv = nk_ref[0], nv_ref[0]
        update(jnp.concatenate([nk[:, hs] for hs in heads], axis=0),
               jnp.concatenate([nv[:, hs] for hs in heads], axis=0), bias_new_ref[...], 0.0)
        lam = _diff_lambda(lam_ref, lam_init)
        inv_l = 1.0 / l_ref[...]
        o = (acc_ref[0:half, :] * inv_l[0:half]
             - lam * (acc_ref[half:2 * half, :] * inv_l[half:2 * half]))
        ms = jnp.mean(o * o, axis=-1, keepdims=True)
        on = o * lax.rsqrt(ms + RMS_EPS) * subg_ref[...] * (1.0 - lam_init)
        on = jnp.concatenate([on[h * t:(h + 1) * t, :] for h in range(H_A)], axis=1)
        o_ref[0] = (on * sg_ref[0].astype(F32)).astype(BF16)


def _sb_decode_kernel(u_ref, q_ref, ck_ref, cv_ref, nk_ref, nv_ref, sg_ref, o_ref, acc_ref,
                      *, past_len, n_new):
    t = q_ref.shape[1]
    qq = _split_halves(q_ref[0])

    def later_sums(log_keep, u):
        hi, lo = _split_bf16(log_keep)
        return _dot(hi, u) + _dot(lo, u)

    row = lax.broadcasted_iota(jnp.int32, (2 * t, NEW_PAD), 0) & (t - 1)
    col = lax.broadcasted_iota(jnp.int32, (2 * t, NEW_PAD), 1)
    before = (col < row) & (col < n_new)
    log_beta, log_keep = _sb_terms(_dot_nt(qq, nk_ref[0]))
    log_keep = jnp.where(before, log_keep, 0.0)
    a = jnp.exp2(log_beta + later_sums(log_keep, u_ref[0:NEW_PAD, 0:NEW_PAD]))
    a = jnp.where(before, a, 0.0)
    acc_ref[...] = _dot(a.astype(BF16), nv_ref[0])
    carry = jnp.sum(log_keep, axis=1, keepdims=True)

    for blk in range(past_len // TK - 1, -1, -1):
        k = ck_ref[0, blk * TK:(blk + 1) * TK, :].astype(BF16)
        v = cv_ref[0, blk * TK:(blk + 1) * TK, :].astype(BF16)
        log_beta, log_keep = _sb_terms(_dot_nt(qq, k))
        a = jnp.exp2(log_beta + later_sums(log_keep, u_ref[...]) + carry)
        acc_ref[...] += _dot(a.astype(BF16), v)
        carry = carry + jnp.sum(log_keep, axis=1, keepdims=True)

    lane = lax.broadcasted_iota(jnp.int32, (t, HEAD_COLS), 1)
    o = jnp.where(lane < DH, acc_ref[0:t, :], acc_ref[t:2 * t, :])
    o_ref[0] = (o * sg_ref[0].astype(F32)).astype(BF16)


def _pad_new(x):
    return jnp.pad(x, ((0, 0), (0, NEW_PAD - x.shape[1]), (0, 0)))


def _decode_specs(t, past_len):
    grp = lambda bi, h: (bi, 0, h)
    qblk = pl.BlockSpec((1, t, HEAD_COLS), grp)
    cblk = pl.BlockSpec((1, past_len, HEAD_COLS), grp)
    nblk = pl.BlockSpec((1, NEW_PAD, HEAD_COLS), grp)
    return qblk, cblk, nblk


def _diff_decode_attention(q, cache_k, cache_v, kb_new, vb_new, sg, lam_rows, subln_g, lam_init):
    b, t, _ = q.shape
    past_len = cache_k.shape[1]
    rows = 2 * H_A * t
    row_head = (jnp.arange(rows) // t) % H_A
    row_frame = jnp.arange(rows) % t
    slope = _alibi_slopes_log2()[row_head][:, None]
    tok, key_head = jnp.arange(DEC_CHUNK * H_A) // H_A, jnp.arange(DEC_CHUNK * H_A) % H_A
    bias_past = jnp.where(key_head[None, :] == row_head[:, None],
                          slope * (row_frame[:, None] - tok[None, :]).astype(F32), -NEG_BIG)
    new_head, new_frame = jnp.arange(H_A * t) // t, jnp.arange(H_A * t) % t
    visible = ((past_len + new_frame[None, :]) // CHUNK) <= ((past_len + row_frame[:, None]) // CHUNK)
    bias_new = jnp.where((new_head[None, :] == row_head[:, None]) & visible,
                         slope * jnp.abs(row_frame[:, None] - new_frame[None, :]).astype(F32), -NEG_BIG)
    whole = lambda bi, c: (bi, 0, 0)
    fixed2 = lambda bi, c: (0, 0)
    frames = pl.BlockSpec((1, t, D_MODEL), whole)
    chunk = pl.BlockSpec((1, DEC_CHUNK * H_A, HEAD_COLS), lambda bi, c: (bi, c, 0))
    return pl.pallas_call(
        functools.partial(_diff_decode_kernel, lam_init=lam_init, past_len=past_len),
        grid=(b, past_len // DEC_CHUNK),
        in_specs=[pl.BlockSpec((8, HEAD_COLS), fixed2), pl.BlockSpec((1, HEAD_COLS), fixed2),
                  pl.BlockSpec((rows, 1), fixed2), pl.BlockSpec(bias_past.shape, fixed2),
                  pl.BlockSpec(bias_new.shape, fixed2), frames, chunk, chunk, frames, frames, frames],
        out_specs=frames,
        out_shape=jax.ShapeDtypeStruct((b, t, D_MODEL), BF16),
        scratch_shapes=[pltpu.VMEM((rows, HEAD_COLS), BF16), pltpu.VMEM((rows, 1), F32),
                        pltpu.VMEM((rows, 1), F32), pltpu.VMEM((rows, HEAD_COLS), F32)],
        compiler_params=_cparams(2),
        name="diff_attn_decode",
    )(lam_rows, subln_g.reshape(1, HEAD_COLS).astype(F32), slope, bias_past, bias_new, q,
      cache_k.reshape(b, past_len * H_A, HEAD_COLS), cache_v.reshape(b, past_len * H_A, HEAD_COLS),
      kb_new, vb_new, sg)


def _sb_decode_attention(q, cache_k, cache_v, kb_new, vb_new, sg):
    b, t, _ = q.shape
    past_len = cache_k.shape[1]
    cache_k = cache_k.reshape(b, past_len, D_MODEL)
    cache_v = cache_v.reshape(b, past_len, D_MODEL)
    qblk, cblk, nblk = _decode_specs(t, past_len)
    return pl.pallas_call(
        functools.partial(_sb_decode_kernel, past_len=past_len, n_new=t),
        grid=(b, N_GROUPS),
        in_specs=[pl.BlockSpec((TK, TK), lambda bi, h: (0, 0)), qblk, cblk, cblk, nblk, nblk, qblk],
        out_specs=qblk,
        out_shape=jax.ShapeDtypeStruct((b, t, D_MODEL), BF16),
        scratch_shapes=[pltpu.VMEM((2 * t, HEAD_COLS), F32)],
        compiler_params=_cparams(2),
        name="sb_attn_decode",
    )(_later_key_matrix(TK), q, cache_k, cache_v, _pad_new(kb_new), _pad_new(vb_new), sg)


def _prompt_layer(x, proj, w_out, attend, v_blocks):
    b, s, _ = x.shape
    q, k, v, kb, vb, sg = _in_proj(x, *proj[:3], tm=TK, v_blocks=v_blocks, q_scale=proj[3])
    a = attend(q, kb, vb, sg)
    y = _out_proj(x.reshape(b * s, D_MODEL), a.reshape(b * s, D_MODEL), w_out, tm=512)
    return y.reshape(b, s, D_MODEL), k, v


def _decode_layer(x, cache_k, cache_v, proj, w_out, attend):
    b, t, _ = x.shape
    n = b * t
    outs = _in_proj(x.reshape(1, n, D_MODEL), *proj[:3], tm=n, v_blocks=False, q_scale=proj[3])
    q, k, v, kb, vb, sg = [o.reshape(b, t, D_MODEL) for o in outs]
    a = attend(q, cache_k, cache_v, kb, vb, sg)
    y = _out_proj(x.reshape(n, D_MODEL), a.reshape(n, D_MODEL), w_out, tm=n)
    return y.reshape(b, t, D_MODEL), k, v


def kernel(x_prompt, x_sample, cache_k_0, cache_v_0, cache_k_1, cache_v_1, norm_g_0, w_in_0, q_norm_0, k_norm_0, lambda_q1_0, lambda_k1_0, lambda_q2_0, lambda_k2_0, subln_g_0, w_out_0, norm_g_1, w_in_1, w_out_1):
    bp, sp, _ = x_prompt.shape
    bs, ss, _ = x_sample.shape

    lam_init = 0.8 - 0.6 * math.exp(-0.3 * 0)
    lam_rows = jnp.zeros((8, HEAD_COLS), F32)
    for r, vec in enumerate((lambda_q1_0, lambda_k1_0, lambda_q2_0, lambda_k2_0)):
        lam_rows = lam_rows.at[r, 0:DH].set(vec.astype(F32))

    diff_prompt = functools.partial(_diff_prompt_attention, lam_rows=lam_rows, subln_g=subln_g_0,
                                    lam_init=lam_init)
    diff_decode = functools.partial(_diff_decode_attention, lam_rows=lam_rows, subln_g=subln_g_0,
                                    lam_init=lam_init)
    w_out_0b, w_out_1b = w_out_0.astype(BF16), w_out_1.astype(BF16)
    proj0 = (norm_g_0, w_in_0.astype(BF16), (q_norm_0, k_norm_0), QK_SCALE * LOG2_E)
    proj1 = (norm_g_1, w_in_1.astype(BF16), None, QK_SCALE * LOG2_E)

    yp, k0p, v0p = _prompt_layer(x_prompt, proj0, w_out_0b, diff_prompt, v_blocks=True)
    ys, k0s, v0s = _decode_layer(x_sample, cache_k_0, cache_v_0, proj0, w_out_0b, diff_decode)
    yp, k1p, v1p = _prompt_layer(yp, proj1, w_out_1b, _sb_prompt_attention, v_blocks=False)
    ys, k1s, v1s = _decode_layer(ys, cache_k_1, cache_v_1, proj1, w_out_1b, _sb_decode_attention)

    return (yp, ys,
            k0p.reshape(bp, sp, H_A, 2 * DH), v0p.reshape(bp, sp, H_A, 2 * DH),
            k0s.reshape(bs, ss, H_A, 2 * DH), v0s.reshape(bs, ss, H_A, 2 * DH),
            k1p.reshape(bp, sp, 2 * H_A, DH), v1p.reshape(bp, sp, 2 * H_A, DH),
            k1s.reshape(bs, ss, 2 * H_A, DH), v1s.reshape(bs, ss, 2 * H_A, DH))
```

```python
import functools
import math

import jax
import jax.numpy as jnp
from jax import lax
from jax.experimental import pallas as pl
from jax.experimental.pallas import tpu as pltpu

F32 = jnp.float32
BF16 = jnp.bfloat16

D_MODEL = 1024
CHUNK = 64
H_A = 8
DH = 64
HEAD_COLS = 128
N_GROUPS = D_MODEL // HEAD_COLS
RMS_EPS = 1e-6
NEG_BIG = -1e30
QK_SCALE = DH ** -0.5
LOG2_E = math.log2(math.e)

TQ = 256
TK = 256
NEW_PAD = 128
SB_GROUPS = 2
DIFF_GROUPS = 4
DEC_CHUNK = 512
DEC_WIDE = 512

VMEM_LIMIT = 48 * 1024 * 1024


def _cparams(n_axes):
    return pltpu.CompilerParams(
        dimension_semantics=("arbitrary",) * n_axes, vmem_limit_bytes=VMEM_LIMIT)


def _dot(a, b):
    return jnp.dot(a, b, preferred_element_type=F32)


def _dot_nt(a, b):
    return lax.dot_general(a, b, (((1,), (1,)), ((), ())), preferred_element_type=F32)


def _group_rms(t, gain_ref, bd_ref):
    sq = (t * t).astype(BF16)
    outs = []
    for c in range(0, D_MODEL, 256):
        ms = _dot(sq[:, c:c + 256], bd_ref[...])
        outs.append(t[:, c:c + 256] * lax.rsqrt(ms + RMS_EPS) * gain_ref[:, c:c + 256])
    return jnp.concatenate(outs, axis=1)


def _inproj_kernel(*refs, qk_norm, v_blocks, q_scale):
    if qk_norm:
        (x_ref, g_ref, w_ref, qg_ref, kg_ref, bd_ref,
         q_ref, k_ref, v_ref, kb_ref, vb_ref, sg_ref) = refs
    else:
        x_ref, g_ref, w_ref, q_ref, k_ref, v_ref, kb_ref, vb_ref, sg_ref = refs
    x = x_ref[0]
    ms = jnp.mean(x * x, axis=-1, keepdims=True)
    h = (x * lax.rsqrt(ms + RMS_EPS) * g_ref[...]).astype(BF16)

    q = _dot(h, w_ref[:, 0:D_MODEL])
    if qk_norm:
        q = _group_rms(q, qg_ref, bd_ref)
    q_ref[0] = (q * q_scale).astype(BF16)

    k = _dot(h, w_ref[:, D_MODEL:2 * D_MODEL])
    if qk_norm:
        k = _group_rms(k, kg_ref, bd_ref)
    k_ref[0] = k
    kb_ref[0] = k.astype(BF16)

    v = _dot(h, w_ref[:, 2 * D_MODEL:3 * D_MODEL])
    v_ref[0] = v
    if v_blocks:
        for g in range(N_GROUPS):
            vb_ref[0, g, 0] = v[:, g * HEAD_COLS:(g + 1) * HEAD_COLS].T.astype(BF16)
    else:
        vb_ref[0] = v.astype(BF16)

    gate = _dot(h, w_ref[:, 3 * D_MODEL:4 * D_MODEL])
    sg_ref[0] = (gate / (1.0 + jnp.exp(-gate))).astype(BF16)


def _in_proj(x, norm_g, w_bf16, qk_gains, tm, v_blocks, q_scale):
    b, s, _ = x.shape
    row = lambda bi, si: (bi, si, 0)
    fixed = lambda bi, si: (0, 0)
    in_specs = [pl.BlockSpec((1, tm, D_MODEL), row),
                pl.BlockSpec((1, D_MODEL), fixed),
                pl.BlockSpec((D_MODEL, 4 * D_MODEL), fixed)]
    args = [x, norm_g.reshape(1, D_MODEL), w_bf16]
    if qk_gains is not None:
        q_gain, k_gain = qk_gains
        r = jnp.arange(256)
        bd = jnp.where((r[:, None] // DH) == (r[None, :] // DH), 1.0 / DH, 0.0).astype(BF16)
        in_specs += [pl.BlockSpec((1, D_MODEL), fixed), pl.BlockSpec((1, D_MODEL), fixed),
                     pl.BlockSpec((256, 256), fixed)]
        args += [jnp.tile(q_gain.astype(F32), D_MODEL // DH).reshape(1, D_MODEL),
                 jnp.tile(k_gain.astype(F32), D_MODEL // DH).reshape(1, D_MODEL), bd]
    out_block = pl.BlockSpec((1, tm, D_MODEL), row)
    shp = lambda dt: jax.ShapeDtypeStruct((b, s, D_MODEL), dt)
    if v_blocks:
        vb_block = pl.BlockSpec((1, N_GROUPS, 1, HEAD_COLS, tm), lambda bi, si: (bi, 0, si, 0, 0))
        vb_shape = jax.ShapeDtypeStruct((b, N_GROUPS, s // tm, HEAD_COLS, tm), BF16)
    else:
        vb_block, vb_shape = out_block, shp(BF16)
    return pl.pallas_call(
        functools.partial(_inproj_kernel, qk_norm=qk_gains is not None, v_blocks=v_blocks, q_scale=q_scale),
        grid=(b, s // tm),
        in_specs=in_specs,
        out_specs=[out_block] * 4 + [vb_block, out_block],
        out_shape=[shp(BF16), shp(F32), shp(F32), shp(BF16), vb_shape, shp(BF16)],
        compiler_params=_cparams(2),
        name="in_proj_qknorm" if qk_gains is not None else "in_proj",
    )(*args)


def _outproj_kernel(x_ref, a_ref, w_ref, y_ref):
    y_ref[...] = x_ref[...] + _dot(a_ref[...], w_ref[...])


def _out_proj(x2d, a_bf16, w_bf16, tm):
    n = x2d.shape[0]
    row = lambda i: (i, 0)
    return pl.pallas_call(
        _outproj_kernel,
        grid=(n // tm,),
        in_specs=[pl.BlockSpec((tm, D_MODEL), row), pl.BlockSpec((tm, D_MODEL), row),
                  pl.BlockSpec((D_MODEL, D_MODEL), lambda i: (0, 0))],
        out_specs=pl.BlockSpec((tm, D_MODEL), row),
        out_shape=jax.ShapeDtypeStruct((n, D_MODEL), F32),
        compiler_params=_cparams(1),
        name="out_proj",
    )(x2d, a_bf16, w_bf16)


def _split_halves(q):
    lane = lax.broadcasted_iota(jnp.int32, q.shape, 1)
    zero = jnp.zeros_like(q)
    return jnp.concatenate([jnp.where(lane < DH, q, zero), jnp.where(lane >= DH, q, zero)], axis=0)


def _diff_lambda(lam_ref, lam_init):
    t1 = jnp.sum(lam_ref[0:1, :] * lam_ref[1:2, :], axis=-1, keepdims=True)
    t2 = jnp.sum(lam_ref[2:3, :] * lam_ref[3:4, :], axis=-1, keepdims=True)
    return jnp.exp(t1) - jnp.exp(t2) + lam_init


def _sb_terms(z2):
    lp = jnp.log2(1.0 + jnp.exp2(-jnp.abs(z2)))
    log_beta = jnp.minimum(z2, 0.0) - lp
    return log_beta, log_beta - z2


def _split_bf16(x):
    hi = x.astype(BF16)
    return hi, (x - hi.astype(F32)).astype(BF16)


def _diff_prompt_kernel(slope_ref, lam_ref, subg_ref, boff_ref, bdiag_ref,
                        q_ref, k_ref, vt_ref, sg_ref, o_ref,
                        m_ref, l_ref, acc_ref, *, lam_init):
    heads = range(DIFF_GROUPS)
    cols = lambda g: slice(g * HEAD_COLS, (g + 1) * HEAD_COLS)
    slopes = [slope_ref[pl.program_id(1) * DIFF_GROUPS + g] for g in heads]
    lam = _diff_lambda(lam_ref, lam_init)

    def q_block(qi, carry):
        q0 = pl.multiple_of(qi * TQ, TQ)
        q = q_ref[0, pl.ds(q0, TQ), :]
        qq = [_split_halves(q[:, cols(g)]) for g in heads]
        m_ref[...] = jnp.full(m_ref.shape, NEG_BIG, F32)
        l_ref[...] = jnp.zeros(l_ref.shape, F32)
        acc_ref[...] = jnp.zeros(acc_ref.shape, F32)

        def block(kj, bias_ref, block_dist):
            k = k_ref[0, pl.ds(pl.multiple_of(kj * TK, TK), TK), :]
            scores = [_dot_nt(k[:, cols(g)], qq[g]) for g in heads]
            p, alpha = [], []
            for g in heads:
                bias = bias_ref[g]
                s = jnp.concatenate([scores[g][:, 0:TQ] - bias, scores[g][:, TQ:2 * TQ] - bias], axis=1)
                shift = slopes[g] * block_dist
                m_prev = m_ref[g]
                m_new = jnp.maximum(m_prev, jnp.max(s, axis=0, keepdims=True) - shift)
                pg = jnp.exp2(s - (m_new + shift))
                ag = jnp.exp2(m_prev - m_new)
                l_ref[g] = ag * l_ref[g] + jnp.sum(pg, axis=0, keepdims=True)
                m_ref[g] = m_new
                p.append(pg.astype(BF16))
                alpha.append(ag)
            for g in heads:
                acc_ref[g] = alpha[g] * acc_ref[g] + _dot(vt_ref[0, g, kj], p[g])

        def past(kj, c):
            block(kj, boff_ref, (qi - kj) * TQ)
            return c

        lax.fori_loop(0, qi, past, 0)
        block(qi, bdiag_ref, 0.0)

        outs = []
        for g in heads:
            inv_l = 1.0 / l_ref[g]
            o_t = (acc_ref[g, :, 0:TQ] * inv_l[:, 0:TQ]
                   - lam * (acc_ref[g, :, TQ:2 * TQ] * inv_l[:, TQ:2 * TQ]))
            ms = jnp.mean(o_t * o_t, axis=0, keepdims=True)
            outs.append((o_t * lax.rsqrt(ms + RMS_EPS)).T * subg_ref[...] * (1.0 - lam_init))
        on = jnp.concatenate(outs, axis=1)
        o_ref[0, pl.ds(q0, TQ), :] = (on * sg_ref[0, pl.ds(q0, TQ), :].astype(F32)).astype(BF16)
        return carry

    lax.fori_loop(0, q_ref.shape[1] // TQ, q_block, 0)


def _sb_prompt_kernel(u_ref, q_ref, k_ref, v_ref, sg_ref, o_ref, c_ref, acc_ref, z_ref):
    lane = lax.broadcasted_iota(jnp.int32, (TQ, HEAD_COLS), 1)
    n_heads = 2 * SB_GROUPS
    cols = lambda hd: slice((hd // 2) * HEAD_COLS, (hd // 2 + 1) * HEAD_COLS)

    def q_block(qi, carry):
        q0 = pl.multiple_of(qi * TQ, TQ)
        q = q_ref[0, pl.ds(q0, TQ), :]
        zero = jnp.zeros((TQ, HEAD_COLS), BF16)
        q_heads = [jnp.where((lane >= DH) == bool(hd % 2), q[:, cols(hd)], zero) for hd in range(n_heads)]
        c_ref[...] = jnp.zeros(c_ref.shape, F32)
        acc_ref[...] = jnp.zeros(acc_ref.shape, F32)

        def scores(kj):
            k = k_ref[0, pl.ds(pl.multiple_of(kj * TK, TK), TK), :]
            return [_dot_nt(q_heads[hd], k[:, cols(hd)]) for hd in range(n_heads)]

        def block(kj, diagonal):
            z = [z_ref[hd] for hd in range(n_heads)]
            z_next = scores(jnp.maximum(kj - 1, 0))
            v = v_ref[0, pl.ds(pl.multiple_of(kj * TK, TK), TK), :]
            if diagonal:
                qry = lax.broadcasted_iota(jnp.int32, (TQ, TK), 0)
                key = lax.broadcasted_iota(jnp.int32, (TQ, TK), 1)
                before = key < qry
            log_beta, later = [], []
            for hd in range(n_heads):
                lb, log_keep = _sb_terms(z[hd])
                if diagonal:
                    log_keep = jnp.where(before, log_keep, 0.0)
                log_beta.append(lb)
                later.append(_dot(log_keep.astype(BF16), u_ref[...]) + c_ref[hd])
                c_ref[hd] += jnp.sum(log_keep, axis=1, keepdims=True)
            for hd in range(n_heads):
                a = jnp.exp2(log_beta[hd] + later[hd])
                if diagonal:
                    a = jnp.where(before, a, 0.0)
                acc_ref[hd] += _dot(a.astype(BF16), v[:, cols(hd)])
            for hd in range(n_heads):
                z_ref[hd] = z_next[hd]

        first = scores(qi)
        for hd in range(n_heads):
            z_ref[hd] = first[hd]
        block(qi, True)

        def earlier(i, c):
            block(qi - 1 - i, False)
            return c

        lax.fori_loop(0, qi, earlier, 0)

        o = jnp.concatenate([jnp.where(lane < DH, acc_ref[2 * g], acc_ref[2 * g + 1])
                             for g in range(SB_GROUPS)], axis=1)
        o_ref[0, pl.ds(q0, TQ), :] = (o * sg_ref[0, pl.ds(q0, TQ), :].astype(F32)).astype(BF16)
        return carry

    lax.fori_loop(0, q_ref.shape[1] // TQ, q_block, 0)


def _later_key_matrix(n):
    r = jnp.arange(n)
    return (r[:, None] > r[None, :]).astype(BF16)


def _alibi_slopes_log2():
    return LOG2_E * 2.0 ** (-8.0 * jnp.arange(1, H_A + 1, dtype=F32) / H_A)


def _diff_prompt_attention(q, kb, vt, sg, lam_rows, subln_g, lam_init):
    b, s, _ = q.shape
    slopes = _alibi_slopes_log2()
    key = jnp.arange(TK)[:, None]
    qry = jnp.arange(TQ)[None, :]
    rel = (qry - key).astype(F32)
    visible = (key // CHUNK) <= (qry // CHUNK)
    bias_off = slopes[:, None, None] * rel[None]
    bias_diag = jnp.where(visible[None], slopes[:, None, None] * jnp.abs(rel)[None], -NEG_BIG)
    grp = lambda bi, h: (bi, 0, h)
    seq = pl.BlockSpec((1, s, DIFF_GROUPS * HEAD_COLS), grp)
    vtspec = pl.BlockSpec((1, DIFF_GROUPS, s // TK, HEAD_COLS, TK), lambda bi, h: (bi, h, 0, 0, 0))
    biasblk = pl.BlockSpec((DIFF_GROUPS, TK, TQ), lambda bi, h: (h, 0, 0))
    fixed2 = lambda bi, h: (0, 0)
    return pl.pallas_call(
        functools.partial(_diff_prompt_kernel, lam_init=lam_init),
        grid=(b, N_GROUPS // DIFF_GROUPS),
        in_specs=[pl.BlockSpec(memory_space=pltpu.SMEM),
                  pl.BlockSpec((8, HEAD_COLS), fixed2), pl.BlockSpec((1, HEAD_COLS), fixed2),
                  biasblk, biasblk, seq, seq, vtspec, seq],
        out_specs=seq,
        out_shape=jax.ShapeDtypeStruct((b, s, D_MODEL), BF16),
        scratch_shapes=[pltpu.VMEM((DIFF_GROUPS, 1, 2 * TQ), F32), pltpu.VMEM((DIFF_GROUPS, 1, 2 * TQ), F32),
                        pltpu.VMEM((DIFF_GROUPS, HEAD_COLS, 2 * TQ), F32)],
        compiler_params=_cparams(2),
        name="diff_attn_prompt",
    )(slopes, lam_rows, subln_g.reshape(1, HEAD_COLS).astype(F32), bias_off, bias_diag, q, kb, vt, sg)


def _sb_prompt_attention(q, kb, vb, sg):
    b, s, _ = q.shape
    seq = pl.BlockSpec((1, s, SB_GROUPS * HEAD_COLS), lambda bi, h: (bi, 0, h))
    n_heads = 2 * SB_GROUPS
    return pl.pallas_call(
        _sb_prompt_kernel,
        grid=(b, N_GROUPS // SB_GROUPS),
        in_specs=[pl.BlockSpec((TK, TK), lambda bi, h: (0, 0)), seq, seq, seq, seq],
        out_specs=seq,
        out_shape=jax.ShapeDtypeStruct((b, s, D_MODEL), BF16),
        scratch_shapes=[pltpu.VMEM((n_heads, TQ, 1), F32), pltpu.VMEM((n_heads, TQ, HEAD_COLS), F32),
                        pltpu.VMEM((n_heads, TQ, TK), F32)],
        compiler_params=_cparams(2),
        name="sb_attn_prompt",
    )(_later_key_matrix(TK), q, kb, vb, sg)


def _diff_decode_kernel(lam_ref, subg_ref, slope_ref, bias_past_ref, bias_new_ref,
                        q_ref, ck_ref, cv_ref, nk_ref, nv_ref, sg_ref, o_ref,
                        m_ref, l_ref, acc_ref, *, lam_init, past_len):
    c = pl.program_id(1)
    t = q_ref.shape[1]
    per_step = DEC_WIDE // HEAD_COLS
    n_half = H_A // per_step
    lane = lax.broadcasted_iota(jnp.int32, (t, DEC_WIDE), 1)

    @pl.when(c == 0)
    def _():
        m_ref[...] = jnp.full(m_ref.shape, NEG_BIG, F32)
        l_ref[...] = jnp.zeros(l_ref.shape, F32)
        acc_ref[...] = jnp.zeros(acc_ref.shape, F32)

    def queries(hh):
        q = q_ref[0, :, hh * DEC_WIDE:(hh + 1) * DEC_WIDE]
        zero = jnp.zeros_like(q)
        keep = lambda comp, h: (lane >= h * HEAD_COLS + comp * DH) & (lane < h * HEAD_COLS + (comp + 1) * DH)
        return jnp.concatenate([jnp.where(keep(comp, h), q, zero)
                                for comp in range(2) for h in range(per_step)], axis=0)

    def update(hh, k, v, bias, shift):
        s = _dot_nt(queries(hh), k) - bias
        m_prev = m_ref[hh]
        m_new = jnp.maximum(m_prev, jnp.max(s, axis=1, keepdims=True) - shift)
        p = jnp.exp2(s - (m_new + shift))
        alpha = jnp.exp2(m_prev - m_new)
        l_ref[hh] = alpha * l_ref[hh] + jnp.sum(p, axis=1, keepdims=True)
        acc_ref[hh] = alpha * acc_ref[hh] + _dot(p.astype(BF16), v)
        m_ref[hh] = m_new

    def cached(ref, hh):
        heads = [ref[0, pl.ds(hh * per_step + h, DEC_CHUNK, stride=H_A), :] for h in range(per_step)]
        return jnp.concatenate(heads, axis=1).astype(BF16)

    chunk_dist = past_len - c * DEC_CHUNK
    for hh in range(n_half):
        update(hh, cached(ck_ref, hh), cached(cv_ref, hh), bias_past_ref[hh], slope_ref[hh] * chunk_dist)

    @pl.when(c == pl.num_programs(1) - 1)
    def _():
        lam = _diff_lambda(lam_ref, lam_init)
        outs = []
        for hh in range(n_half):
            cols = slice(hh * DEC_WIDE, (hh + 1) * DEC_WIDE)
            update(hh, nk_ref[0, :, cols], nv_ref[0, :, cols], bias_new_ref[hh], 0.0)
            inv_l = 1.0 / l_ref[hh]
            for h in range(per_step):
                r1 = slice(h * t, (h + 1) * t)
                r2 = slice((per_step + h) * t, (per_step + h + 1) * t)
                own = slice(h * HEAD_COLS, (h + 1) * HEAD_COLS)
                o = acc_ref[hh, r1, own] * inv_l[r1] - lam * (acc_ref[hh, r2, own] * inv_l[r2])
                ms = jnp.mean(o * o, axis=-1, keepdims=True)
                outs.append(o * lax.rsqrt(ms + RMS_EPS) * subg_ref[...] * (1.0 - lam_init))
        on = jnp.concatenate(outs, axis=1)
        o_ref[0] = (on * sg_ref[0].astype(F32)).astype(BF16)


def _sb_decode_kernel(u_ref, q_ref, ck_ref, cv_ref, nk_ref, nv_ref, sg_ref, o_ref, *, past_len, n_new):
    t = q_ref.shape[1]
    n_heads = DEC_WIDE // DH
    lane = lax.broadcasted_iota(jnp.int32, (t, DEC_WIDE), 1)
    own = [(lane >= h * DH) & (lane < (h + 1) * DH) for h in range(n_heads)]
    q = q_ref[0]
    zero = jnp.zeros_like(q)
    qq = jnp.concatenate([jnp.where(own[h], q, zero) for h in range(n_heads)], axis=0)
    rows = n_heads * t

    def later_sums(log_keep, u):
        hi, lo = _split_bf16(log_keep)
        return _dot(hi, u) + _dot(lo, u)

    frame = lax.broadcasted_iota(jnp.int32, (rows, NEW_PAD), 0) & (t - 1)
    col = lax.broadcasted_iota(jnp.int32, (rows, NEW_PAD), 1)
    before = (col < frame) & (col < n_new)
    log_beta, log_keep = _sb_terms(_dot_nt(qq, nk_ref[0]))
    log_keep = jnp.where(before, log_keep, 0.0)
    a = jnp.exp2(log_beta + later_sums(log_keep, u_ref[0:NEW_PAD, 0:NEW_PAD]))
    acc = _dot(jnp.where(before, a, 0.0).astype(BF16), nv_ref[0])
    carry = jnp.sum(log_keep, axis=1, keepdims=True)

    log_beta, log_keep = _sb_terms(_dot_nt(qq, ck_ref[0].astype(BF16)))
    blocks = [slice(blk * TK, (blk + 1) * TK) for blk in range(past_len // TK)]
    later = [later_sums(log_keep[:, s], u_ref[...]) for s in blocks]
    weights = [None] * len(blocks)
    for blk in reversed(range(len(blocks))):
        weights[blk] = jnp.exp2(log_beta[:, blocks[blk]] + later[blk] + carry).astype(BF16)
        carry = carry + jnp.sum(log_keep[:, blocks[blk]], axis=1, keepdims=True)
    acc = acc + _dot(jnp.concatenate(weights, axis=1), cv_ref[0].astype(BF16))

    o = jnp.zeros((t, DEC_WIDE), F32)
    for h in range(n_heads):
        o = jnp.where(own[h], acc[h * t:(h + 1) * t, :], o)
    o_ref[0] = (o * sg_ref[0].astype(F32)).astype(BF16)


def _pad_new(x):
    return jnp.pad(x, ((0, 0), (0, NEW_PAD - x.shape[1]), (0, 0)))


def _diff_decode_attention(q, cache_k, cache_v, kb_new, vb_new, sg, lam_rows, subln_g, lam_init):
    b, t, _ = q.shape
    past_len = cache_k.shape[1]
    per_step = DEC_WIDE // HEAD_COLS
    n_half = H_A // per_step
    rows = 2 * per_step * t
    row_head = (jnp.arange(rows) // t) % per_step
    row_frame = (jnp.arange(rows) % t)[:, None]
    heads = jnp.arange(n_half)[:, None] * per_step + row_head[None, :]
    slope = _alibi_slopes_log2()[heads][:, :, None]
    bias_past = slope * (row_frame - jnp.arange(DEC_CHUNK)[None, :]).astype(F32)[None]
    new_frame = jnp.arange(NEW_PAD)[None, :]
    visible = ((past_len + new_frame) // CHUNK) <= ((past_len + row_frame) // CHUNK)
    bias_new = jnp.where(((new_frame < t) & visible)[None],
                         slope * jnp.abs(row_frame - new_frame).astype(F32)[None], -NEG_BIG)
    whole = lambda bi, c: (bi, 0, 0)
    fixed2 = lambda bi, c: (0, 0)
    fixed3 = lambda bi, c: (0, 0, 0)
    frames = pl.BlockSpec((1, t, D_MODEL), whole)
    padded = pl.BlockSpec((1, NEW_PAD, D_MODEL), whole)
    chunk = pl.BlockSpec((1, DEC_CHUNK * H_A, HEAD_COLS), lambda bi, c: (bi, c, 0))
    return pl.pallas_call(
        functools.partial(_diff_decode_kernel, lam_init=lam_init, past_len=past_len),
        grid=(b, past_len // DEC_CHUNK),
        in_specs=[pl.BlockSpec((8, HEAD_COLS), fixed2), pl.BlockSpec((1, HEAD_COLS), fixed2),
                  pl.BlockSpec(slope.shape, fixed3), pl.BlockSpec(bias_past.shape, fixed3),
                  pl.BlockSpec(bias_new.shape, fixed3), frames, chunk, chunk, padded, padded, frames],
        out_specs=frames,
        out_shape=jax.ShapeDtypeStruct((b, t, D_MODEL), BF16),
        scratch_shapes=[pltpu.VMEM((n_half, rows, 1), F32), pltpu.VMEM((n_half, rows, 1), F32),
                        pltpu.VMEM((n_half, rows, DEC_WIDE), F32)],
        compiler_params=_cparams(2),
        name="diff_attn_decode",
    )(lam_rows, subln_g.reshape(1, HEAD_COLS).astype(F32), slope, bias_past, bias_new, q,
      cache_k.reshape(b, past_len * H_A, HEAD_COLS), cache_v.reshape(b, past_len * H_A, HEAD_COLS),
      _pad_new(kb_new), _pad_new(vb_new), sg)


def _sb_decode_attention(q, cache_k, cache_v, kb_new, vb_new, sg):
    b, t, _ = q.shape
    past_len = cache_k.shape[1]
    cache_k = cache_k.reshape(b, past_len, D_MODEL)
    cache_v = cache_v.reshape(b, past_len, D_MODEL)
    grp = lambda bi, h: (bi, 0, h)
    qblk = pl.BlockSpec((1, t, DEC_WIDE), grp)
    cblk = pl.BlockSpec((1, past_len, DEC_WIDE), grp)
    nblk = pl.BlockSpec((1, NEW_PAD, DEC_WIDE), grp)
    return pl.pallas_call(
        functools.partial(_sb_decode_kernel, past_len=past_len, n_new=t),
        grid=(b, D_MODEL // DEC_WIDE),
        in_specs=[pl.BlockSpec((TK, TK), lambda bi, h: (0, 0)), qblk, cblk, cblk, nblk, nblk, qblk],
        out_specs=qblk,
        out_shape=jax.ShapeDtypeStruct((b, t, D_MODEL), BF16),
        compiler_params=_cparams(2),
        name="sb_attn_decode",
    )(_later_key_matrix(TK), q, cache_k, cache_v, _pad_new(kb_new), _pad_new(vb_new), sg)


def _prompt_layer(x, proj, w_out, attend, v_blocks):
    b, s, _ = x.shape
    q, k, v, kb, vb, sg = _in_proj(x, *proj[:3], tm=TK, v_blocks=v_blocks, q_scale=proj[3])
    a = attend(q, kb, vb, sg)
    y = _out_proj(x.reshape(b * s, D_MODEL), a.reshape(b * s, D_MODEL), w_out, tm=512)
    return y.reshape(b, s, D_MODEL), k, v


def _decode_layer(x, cache_k, cache_v, proj, w_out, attend):
    b, t, _ = x.shape
    n = b * t
    outs = _in_proj(x.reshape(1, n, D_MODEL), *proj[:3], tm=n, v_blocks=False, q_scale=proj[3])
    q, k, v, kb, vb, sg = [o.reshape(b, t, D_MODEL) for o in outs]
    a = attend(q, cache_k, cache_v, kb, vb, sg)
    y = _out_proj(x.reshape(n, D_MODEL), a.reshape(n, D_MODEL), w_out, tm=n)
    return y.reshape(b, t, D_MODEL), k, v


def kernel(x_prompt, x_sample, cache_k_0, cache_v_0, cache_k_1, cache_v_1, norm_g_0, w_in_0, q_norm_0, k_norm_0, lambda_q1_0, lambda_k1_0, lambda_q2_0, lambda_k2_0, subln_g_0, w_out_0, norm_g_1, w_in_1, w_out_1):
    bp, sp, _ = x_prompt.shape
    bs, ss, _ = x_sample.shape

    lam_init = 0.8 - 0.6 * math.exp(-0.3 * 0)
    lam_rows = jnp.zeros((8, HEAD_COLS), F32)
    for r, vec in enumerate((lambda_q1_0, lambda_k1_0, lambda_q2_0, lambda_k2_0)):
        lam_rows = lam_rows.at[r, 0:DH].set(vec.astype(F32))

    diff_prompt = functools.partial(_diff_prompt_attention, lam_rows=lam_rows, subln_g=subln_g_0,
                                    lam_init=lam_init)
    diff_decode = functools.partial(_diff_decode_attention, lam_rows=lam_rows, subln_g=subln_g_0,
                                    lam_init=lam_init)
    w_out_0b, w_out_1b = w_out_0.astype(BF16), w_out_1.astype(BF16)
    proj0 = (norm_g_0, w_in_0.astype(BF16), (q_norm_0, k_norm_0), QK_SCALE * LOG2_E)
    proj1 = (norm_g_1, w_in_1.astype(BF16), None, QK_SCALE * LOG2_E)

    yp, k0p, v0p = _prompt_layer(x_prompt, proj0, w_out_0b, diff_prompt, v_blocks=True)
    ys, k0s, v0s = _decode_layer(x_sample, cache_k_0, cache_v_0, proj0, w_out_0b, diff_decode)
    yp, k1p, v1p = _prompt_layer(yp, proj1, w_out_1b, _sb_prompt_attention, v_blocks=False)
    ys, k1s, v1s = _decode_layer(ys, cache_k_1, cache_v_1, proj1, w_out_1b, _sb_decode_attention)

    return (yp, ys,
            k0p.reshape(bp, sp, H_A, 2 * DH), v0p.reshape(bp, sp, H_A, 2 * DH),
            k0s.reshape(bs, ss, H_A, 2 * DH), v0s.reshape(bs, ss, H_A, 2 * DH),
            k1p.reshape(bp, sp, 2 * H_A, DH), v1p.reshape(bp, sp, 2 * H_A, DH),
            k1s.reshape(bs, ss, 2 * H_A, DH), v1s.reshape(bs, ss, 2 * H_A, DH))
```

```python
import functools
import math

import jax
import jax.numpy as jnp
from jax import lax
from jax.experimental import pallas as pl
from jax.experimental.pallas import tpu as pltpu

F32 = jnp.float32
BF16 = jnp.bfloat16

D_MODEL = 1024
CHUNK = 64
H_A = 8
DH = 64
HEAD_COLS = 128
N_GROUPS = D_MODEL // HEAD_COLS
RMS_EPS = 1e-6
NEG_BIG = -1e30
QK_SCALE = DH ** -0.5
LOG2_E = math.log2(math.e)

TQ = 256
TK = 256
NEW_PAD = 128
SB_GROUPS = 2
DIFF_GROUPS = 4
DEC_CHUNK = 512
DEC_WIDE = 512

VMEM_LIMIT = 48 * 1024 * 1024


def _cparams(n_axes):
    return pltpu.CompilerParams(
        dimension_semantics=("arbitrary",) * n_axes, vmem_limit_bytes=VMEM_LIMIT)


def _dot(a, b):
    return jnp.dot(a, b, preferred_element_type=F32)


def _dot_nt(a, b):
    return lax.dot_general(a, b, (((1,), (1,)), ((), ())), preferred_element_type=F32)


def _group_rms(t, gain_ref, bd_ref):
    sq = (t * t).astype(BF16)
    outs = []
    for c in range(0, D_MODEL, 256):
        ms = _dot(sq[:, c:c + 256], bd_ref[...])
        outs.append(t[:, c:c + 256] * lax.rsqrt(ms + RMS_EPS) * gain_ref[:, c:c + 256])
    return jnp.concatenate(outs, axis=1)


def _inproj_kernel(*refs, qk_norm, v_blocks, q_scale):
    if qk_norm:
        (x_ref, g_ref, w_ref, qg_ref, kg_ref, bd_ref,
         q_ref, k_ref, v_ref, kb_ref, vb_ref, sg_ref) = refs
    else:
        x_ref, g_ref, w_ref, q_ref, k_ref, v_ref, kb_ref, vb_ref, sg_ref = refs
    x = x_ref[0]
    ms = jnp.mean(x * x, axis=-1, keepdims=True)
    h = (x * lax.rsqrt(ms + RMS_EPS) * g_ref[...]).astype(BF16)

    q = _dot(h, w_ref[:, 0:D_MODEL])
    if qk_norm:
        q = _group_rms(q, qg_ref, bd_ref)
    q_ref[0] = (q * q_scale).astype(BF16)

    k = _dot(h, w_ref[:, D_MODEL:2 * D_MODEL])
    if qk_norm:
        k = _group_rms(k, kg_ref, bd_ref)
    k_ref[0] = k
    kb_ref[0] = k.astype(BF16)

    v = _dot(h, w_ref[:, 2 * D_MODEL:3 * D_MODEL])
    v_ref[0] = v
    if v_blocks:
        for g in range(N_GROUPS):
            vb_ref[0, g, 0] = v[:, g * HEAD_COLS:(g + 1) * HEAD_COLS].T.astype(BF16)
    else:
        vb_ref[0] = v.astype(BF16)

    gate = _dot(h, w_ref[:, 3 * D_MODEL:4 * D_MODEL])
    sg_ref[0] = (gate / (1.0 + jnp.exp(-gate))).astype(BF16)


def _in_proj(x, norm_g, w_bf16, qk_gains, tm, v_blocks, q_scale):
    b, s, _ = x.shape
    row = lambda bi, si: (bi, si, 0)
    fixed = lambda bi, si: (0, 0)
    in_specs = [pl.BlockSpec((1, tm, D_MODEL), row),
                pl.BlockSpec((1, D_MODEL), fixed),
                pl.BlockSpec((D_MODEL, 4 * D_MODEL), fixed)]
    args = [x, norm_g.reshape(1, D_MODEL), w_bf16]
    if qk_gains is not None:
        q_gain, k_gain = qk_gains
        r = jnp.arange(256)
        bd = jnp.where((r[:, None] // DH) == (r[None, :] // DH), 1.0 / DH, 0.0).astype(BF16)
        in_specs += [pl.BlockSpec((1, D_MODEL), fixed), pl.BlockSpec((1, D_MODEL), fixed),
                     pl.BlockSpec((256, 256), fixed)]
        args += [jnp.tile(q_gain.astype(F32), D_MODEL // DH).reshape(1, D_MODEL),
                 jnp.tile(k_gain.astype(F32), D_MODEL // DH).reshape(1, D_MODEL), bd]
    out_block = pl.BlockSpec((1, tm, D_MODEL), row)
    shp = lambda dt: jax.ShapeDtypeStruct((b, s, D_MODEL), dt)
    if v_blocks:
        vb_block = pl.BlockSpec((1, N_GROUPS, 1, HEAD_COLS, tm), lambda bi, si: (bi, 0, si, 0, 0))
        vb_shape = jax.ShapeDtypeStruct((b, N_GROUPS, s // tm, HEAD_COLS, tm), BF16)
    else:
        vb_block, vb_shape = out_block, shp(BF16)
    return pl.pallas_call(
        functools.partial(_inproj_kernel, qk_norm=qk_gains is not None, v_blocks=v_blocks, q_scale=q_scale),
        grid=(b, s // tm),
        in_specs=in_specs,
        out_specs=[out_block] * 4 + [vb_block, out_block],
        out_shape=[shp(BF16), shp(F32), shp(F32), shp(BF16), vb_shape, shp(BF16)],
        compiler_params=_cparams(2),
        name="in_proj_qknorm" if qk_gains is not None else "in_proj",
    )(*args)


def _outproj_kernel(x_ref, a_ref, w_ref, y_ref):
    y_ref[...] = x_ref[...] + _dot(a_ref[...], w_ref[...])


def _out_proj(x2d, a_bf16, w_bf16, tm):
    n = x2d.shape[0]
    row = lambda i: (i, 0)
    return pl.pallas_call(
        _outproj_kernel,
        grid=(n // tm,),
        in_specs=[pl.BlockSpec((tm, D_MODEL), row), pl.BlockSpec((tm, D_MODEL), row),
                  pl.BlockSpec((D_MODEL, D_MODEL), lambda i: (0, 0))],
        out_specs=pl.BlockSpec((tm, D_MODEL), row),
        out_shape=jax.ShapeDtypeStruct((n, D_MODEL), F32),
        compiler_params=_cparams(1),
        name="out_proj",
    )(x2d, a_bf16, w_bf16)


def _split_halves(q):
    lane = lax.broadcasted_iota(jnp.int32, q.shape, 1)
    zero = jnp.zeros_like(q)
    return jnp.concatenate([jnp.where(lane < DH, q, zero), jnp.where(lane >= DH, q, zero)], axis=0)


def _diff_lambda(lam_ref, lam_init):
    t1 = jnp.sum(lam_ref[0:1, :] * lam_ref[1:2, :], axis=-1, keepdims=True)
    t2 = jnp.sum(lam_ref[2:3, :] * lam_ref[3:4, :], axis=-1, keepdims=True)
    return jnp.exp(t1) - jnp.exp(t2) + lam_init


def _sb_terms(z2):
    lp = jnp.log2(1.0 + jnp.exp2(-jnp.abs(z2)))
    log_beta = jnp.minimum(z2, 0.0) - lp
    return log_beta, log_beta - z2


def _split_bf16(x):
    hi = x.astype(BF16)
    return hi, (x - hi.astype(F32)).astype(BF16)


def _diff_prompt_kernel(slope_ref, lam_ref, subg_ref, boff_ref, bdiag_ref,
                        q_ref, k_ref, vt_ref, sg_ref, o_ref,
                        m_ref, l_ref, acc_ref, *, lam_init):
    heads = range(DIFF_GROUPS)
    cols = lambda g: slice(g * HEAD_COLS, (g + 1) * HEAD_COLS)
    slopes = [slope_ref[pl.program_id(1) * DIFF_GROUPS + g] for g in heads]
    lam = _diff_lambda(lam_ref, lam_init)

    def q_block(qi, carry):
        q0 = pl.multiple_of(qi * TQ, TQ)
        q = q_ref[0, pl.ds(q0, TQ), :]
        qq = [_split_halves(q[:, cols(g)]) for g in heads]
        m_ref[...] = jnp.full(m_ref.shape, NEG_BIG, F32)
        l_ref[...] = jnp.zeros(l_ref.shape, F32)
        acc_ref[...] = jnp.zeros(acc_ref.shape, F32)

        def block(kj, bias_ref, block_dist):
            k = k_ref[0, pl.ds(pl.multiple_of(kj * TK, TK), TK), :]
            scores = [_dot_nt(k[:, cols(g)], qq[g]) for g in heads]
            p, alpha = [], []
            for g in heads:
                bias = bias_ref[g]
                s = jnp.concatenate([scores[g][:, 0:TQ] - bias, scores[g][:, TQ:2 * TQ] - bias], axis=1)
                shift = slopes[g] * block_dist
                m_prev = m_ref[g]
                m_new = jnp.maximum(m_prev, jnp.max(s, axis=0, keepdims=True) - shift)
                pg = jnp.exp2(s - (m_new + shift))
                ag = jnp.exp2(m_prev - m_new)
                m_ref[g] = m_new
                p.append(pg.astype(BF16))
                alpha.append(ag)
            ones = jnp.ones((16, TK), BF16)
            for g in heads:
                pv = _dot(jnp.concatenate([vt_ref[0, g, kj], ones], axis=0), p[g])
                acc_ref[g] = alpha[g] * acc_ref[g] + pv[0:HEAD_COLS]
                l_ref[g] = alpha[g] * l_ref[g] + pv[HEAD_COLS:HEAD_COLS + 1]

        def past(kj, c):
            block(kj, boff_ref, (qi - kj) * TQ)
            return c

        lax.fori_loop(0, qi, past, 0)
        block(qi, bdiag_ref, 0.0)

        outs = []
        for g in heads:
            inv_l = 1.0 / l_ref[g]
            o_t = (acc_ref[g, :, 0:TQ] * inv_l[:, 0:TQ]
                   - lam * (acc_ref[g, :, TQ:2 * TQ] * inv_l[:, TQ:2 * TQ]))
            ms = jnp.mean(o_t * o_t, axis=0, keepdims=True)
            outs.append((o_t * lax.rsqrt(ms + RMS_EPS)).T * subg_ref[...] * (1.0 - lam_init))
        on = jnp.concatenate(outs, axis=1)
        o_ref[0, pl.ds(q0, TQ), :] = (on * sg_ref[0, pl.ds(q0, TQ), :].astype(F32)).astype(BF16)
        return carry

    lax.fori_loop(0, q_ref.shape[1] // TQ, q_block, 0)


def _sb_prompt_kernel(u_ref, q_ref, k_ref, v_ref, sg_ref, o_ref, c_ref, acc_ref, z_ref):
    lane = lax.broadcasted_iota(jnp.int32, (TQ, HEAD_COLS), 1)
    n_heads = 2 * SB_GROUPS
    cols = lambda hd: slice((hd // 2) * HEAD_COLS, (hd // 2 + 1) * HEAD_COLS)

    def q_block(qi, carry):
        q0 = pl.multiple_of(qi * TQ, TQ)
        q = q_ref[0, pl.ds(q0, TQ), :]
        zero = jnp.zeros((TQ, HEAD_COLS), BF16)
        q_heads = [jnp.where((lane >= DH) == bool(hd % 2), q[:, cols(hd)], zero) for hd in range(n_heads)]
        c_ref[...] = jnp.zeros(c_ref.shape, F32)
        acc_ref[...] = jnp.zeros(acc_ref.shape, F32)

        def scores(kj):
            k = k_ref[0, pl.ds(pl.multiple_of(kj * TK, TK), TK), :]
            return [_dot_nt(q_heads[hd], k[:, cols(hd)]) for hd in range(n_heads)]

        def block(kj, diagonal):
            z = [z_ref[hd] for hd in range(n_heads)]
            z_next = scores(jnp.maximum(kj - 1, 0))
            v = v_ref[0, pl.ds(pl.multiple_of(kj * TK, TK), TK), :]
            if diagonal:
                qry = lax.broadcasted_iota(jnp.int32, (TQ, TK), 0)
                key = lax.broadcasted_iota(jnp.int32, (TQ, TK), 1)
                before = key < qry
            log_beta, later = [], []
            for hd in range(n_heads):
                lb, log_keep = _sb_terms(z[hd])
                if diagonal:
                    log_keep = jnp.where(before, log_keep, 0.0)
                log_beta.append(lb)
                later.append(_dot(log_keep.astype(BF16), u_ref[...]) + c_ref[hd])
                c_ref[hd] += jnp.sum(log_keep, axis=1, keepdims=True)
            for hd in range(n_heads):
                a = jnp.exp2(log_beta[hd] + later[hd])
                if diagonal:
                    a = jnp.where(before, a, 0.0)
                acc_ref[hd] += _dot(a.astype(BF16), v[:, cols(hd)])
            for hd in range(n_heads):
                z_ref[hd] = z_next[hd]

        first = scores(qi)
        for hd in range(n_heads):
            z_ref[hd] = first[hd]
        block(qi, True)

        def earlier(i, c):
            block(qi - 1 - i, False)
            return c

        lax.fori_loop(0, qi, earlier, 0)

        o = jnp.concatenate([jnp.where(lane < DH, acc_ref[2 * g], acc_ref[2 * g + 1])
                             for g in range(SB_GROUPS)], axis=1)
        o_ref[0, pl.ds(q0, TQ), :] = (o * sg_ref[0, pl.ds(q0, TQ), :].astype(F32)).astype(BF16)
        return carry

    lax.fori_loop(0, q_ref.shape[1] // TQ, q_block, 0)


def _later_key_matrix(n):
    r = jnp.arange(n)
    return (r[:, None] > r[None, :]).astype(BF16)


def _alibi_slopes_log2():
    return LOG2_E * 2.0 ** (-8.0 * jnp.arange(1, H_A + 1, dtype=F32) / H_A)


def _diff_prompt_attention(q, kb, vt, sg, lam_rows, subln_g, lam_init):
    b, s, _ = q.shape
    slopes = _alibi_slopes_log2()
    key = jnp.arange(TK)[:, None]
    qry = jnp.arange(TQ)[None, :]
    rel = (qry - key).astype(F32)
    visible = (key // CHUNK) <= (qry // CHUNK)
    bias_off = slopes[:, None, None] * rel[None]
    bias_diag = jnp.where(visible[None], slopes[:, None, None] * jnp.abs(rel)[None], -NEG_BIG)
    grp = lambda bi, h: (bi, 0, h)
    seq = pl.BlockSpec((1, s, DIFF_GROUPS * HEAD_COLS), grp)
    vtspec = pl.BlockSpec((1, DIFF_GROUPS, s // TK, HEAD_COLS, TK), lambda bi, h: (bi, h, 0, 0, 0))
    biasblk = pl.BlockSpec((DIFF_GROUPS, TK, TQ), lambda bi, h: (h, 0, 0))
    fixed2 = lambda bi, h: (0, 0)
    return pl.pallas_call(
        functools.partial(_diff_prompt_kernel, lam_init=lam_init),
        grid=(b, N_GROUPS // DIFF_GROUPS),
        in_specs=[pl.BlockSpec(memory_space=pltpu.SMEM),
                  pl.BlockSpec((8, HEAD_COLS), fixed2), pl.BlockSpec((1, HEAD_COLS), fixed2),
                  biasblk, biasblk, seq, seq, vtspec, seq],
        out_specs=seq,
        out_shape=jax.ShapeDtypeStruct((b, s, D_MODEL), BF16),
        scratch_shapes=[pltpu.VMEM((DIFF_GROUPS, 1, 2 * TQ), F32), pltpu.VMEM((DIFF_GROUPS, 1, 2 * TQ), F32),
                        pltpu.VMEM((DIFF_GROUPS, HEAD_COLS, 2 * TQ), F32)],
        compiler_params=_cparams(2),
        name="diff_attn_prompt",
    )(slopes, lam_rows, subln_g.reshape(1, HEAD_COLS).astype(F32), bias_off, bias_diag, q, kb, vt, sg)


def _sb_prompt_attention(q, kb, vb, sg):
    b, s, _ = q.shape
    seq = pl.BlockSpec((1, s, SB_GROUPS * HEAD_COLS), lambda bi, h: (bi, 0, h))
    n_heads = 2 * SB_GROUPS
    return pl.pallas_call(
        _sb_prompt_kernel,
        grid=(b, N_GROUPS // SB_GROUPS),
        in_specs=[pl.BlockSpec((TK, TK), lambda bi, h: (0, 0)), seq, seq, seq, seq],
        out_specs=seq,
        out_shape=jax.ShapeDtypeStruct((b, s, D_MODEL), BF16),
        scratch_shapes=[pltpu.VMEM((n_heads, TQ, 1), F32), pltpu.VMEM((n_heads, TQ, HEAD_COLS), F32),
                        pltpu.VMEM((n_heads, TQ, TK), F32)],
        compiler_params=_cparams(2),
        name="sb_attn_prompt",
    )(_later_key_matrix(TK), q, kb, vb, sg)


def _diff_decode_kernel(lam_ref, subg_ref, slope_ref, bias_past_ref, bias_new_ref,
                        q_ref, ck_ref, cv_ref, nk_ref, nv_ref, sg_ref, o_ref,
                        m_ref, l_ref, acc_ref, *, lam_init, past_len):
    c = pl.program_id(1)
    t = q_ref.shape[1]
    per_step = DEC_WIDE // HEAD_COLS
    n_half = H_A // per_step
    lane = lax.broadcasted_iota(jnp.int32, (t, DEC_WIDE), 1)

    @pl.when(c == 0)
    def _():
        m_ref[...] = jnp.full(m_ref.shape, NEG_BIG, F32)
        l_ref[...] = jnp.zeros(l_ref.shape, F32)
        acc_ref[...] = jnp.zeros(acc_ref.shape, F32)

    def queries(hh):
        q = q_ref[0, :, hh * DEC_WIDE:(hh + 1) * DEC_WIDE]
        zero = jnp.zeros_like(q)
        keep = lambda comp, h: (lane >= h * HEAD_COLS + comp * DH) & (lane < h * HEAD_COLS + (comp + 1) * DH)
        return jnp.concatenate([jnp.where(keep(comp, h), q, zero)
                                for comp in range(2) for h in range(per_step)], axis=0)

    def update(hh, k, v, bias, shift):
        s = _dot_nt(queries(hh), k) - bias
        m_prev = m_ref[hh]
        m_new = jnp.maximum(m_prev, jnp.max(s, axis=1, keepdims=True) - shift)
        p = jnp.exp2(s - (m_new + shift))
        alpha = jnp.exp2(m_prev - m_new)
        l_ref[hh] = alpha * l_ref[hh] + jnp.sum(p, axis=1, keepdims=True)
        acc_ref[hh] = alpha * acc_ref[hh] + _dot(p.astype(BF16), v)
        m_ref[hh] = m_new

    def cached(ref, hh):
        heads = [ref[0, pl.ds(hh * per_step + h, DEC_CHUNK, stride=H_A), :] for h in range(per_step)]
        return jnp.concatenate(heads, axis=1).astype(BF16)

    chunk_dist = past_len - c * DEC_CHUNK
    for hh in range(n_half):
        update(hh, cached(ck_ref, hh), cached(cv_ref, hh), bias_past_ref[hh], slope_ref[hh] * chunk_dist)

    @pl.when(c == pl.num_programs(1) - 1)
    def _():
        lam = _diff_lambda(lam_ref, lam_init)
        outs = []
        for hh in range(n_half):
            cols = slice(hh * DEC_WIDE, (hh + 1) * DEC_WIDE)
            update(hh, nk_ref[0, :, cols], nv_ref[0, :, cols], bias_new_ref[hh], 0.0)
            inv_l = 1.0 / l_ref[hh]
            for h in range(per_step):
                r1 = slice(h * t, (h + 1) * t)
                r2 = slice((per_step + h) * t, (per_step + h + 1) * t)
                own = slice(h * HEAD_COLS, (h + 1) * HEAD_COLS)
                o = acc_ref[hh, r1, own] * inv_l[r1] - lam * (acc_ref[hh, r2, own] * inv_l[r2])
                ms = jnp.mean(o * o, axis=-1, keepdims=True)
                outs.append(o * lax.rsqrt(ms + RMS_EPS) * subg_ref[...] * (1.0 - lam_init))
        on = jnp.concatenate(outs, axis=1)
        o_ref[0] = (on * sg_ref[0].astype(F32)).astype(BF16)


def _sb_decode_kernel(u_ref, q_ref, ck_ref, cv_ref, nk_ref, nv_ref, sg_ref, o_ref, *, past_len, n_new):
    t = q_ref.shape[1]
    n_heads = DEC_WIDE // DH
    lane = lax.broadcasted_iota(jnp.int32, (t, DEC_WIDE), 1)
    own = [(lane >= h * DH) & (lane < (h + 1) * DH) for h in range(n_heads)]
    q = q_ref[0]
    zero = jnp.zeros_like(q)
    qq = jnp.concatenate([jnp.where(own[h], q, zero) for h in range(n_heads)], axis=0)
    rows = n_heads * t

    def later_sums(log_keep, u):
        hi, lo = _split_bf16(log_keep)
        return _dot(hi, u) + _dot(lo, u)

    frame = lax.broadcasted_iota(jnp.int32, (rows, NEW_PAD), 0) & (t - 1)
    col = lax.broadcasted_iota(jnp.int32, (rows, NEW_PAD), 1)
    before = (col < frame) & (col < n_new)
    log_beta, log_keep = _sb_terms(_dot_nt(qq, nk_ref[0]))
    log_keep = jnp.where(before, log_keep, 0.0)
    a = jnp.exp2(log_beta + later_sums(log_keep, u_ref[0:NEW_PAD, 0:NEW_PAD]))
    acc = _dot(jnp.where(before, a, 0.0).astype(BF16), nv_ref[0])
    carry = jnp.sum(log_keep, axis=1, keepdims=True)

    log_beta, log_keep = _sb_terms(_dot_nt(qq, ck_ref[0]))
    blocks = [slice(blk * TK, (blk + 1) * TK) for blk in range(past_len // TK)]
    later = [later_sums(log_keep[:, s], u_ref[...]) for s in blocks]
    weights = [None] * len(blocks)
    for blk in reversed(range(len(blocks))):
        weights[blk] = jnp.exp2(log_beta[:, blocks[blk]] + later[blk] + carry).astype(BF16)
        carry = carry + jnp.sum(log_keep[:, blocks[blk]], axis=1, keepdims=True)
    acc = acc + _dot(jnp.concatenate(weights, axis=1), cv_ref[0])

    o = jnp.zeros((t, DEC_WIDE), F32)
    for h in range(n_heads):
        o = jnp.where(own[h], acc[h * t:(h + 1) * t, :], o)
    o_ref[0] = (o * sg_ref[0].astype(F32)).astype(BF16)


def _pad_new(x):
    return jnp.pad(x, ((0, 0), (0, NEW_PAD - x.shape[1]), (0, 0)))


def _diff_decode_attention(q, cache_k, cache_v, kb_new, vb_new, sg, lam_rows, subln_g, lam_init):
    b, t, _ = q.shape
    past_len = cache_k.shape[1]
    per_step = DEC_WIDE // HEAD_COLS
    n_half = H_A // per_step
    rows = 2 * per_step * t
    row_head = (jnp.arange(rows) // t) % per_step
    row_frame = (jnp.arange(rows) % t)[:, None]
    heads = jnp.arange(n_half)[:, None] * per_step + row_head[None, :]
    slope = _alibi_slopes_log2()[heads][:, :, None]
    bias_past = slope * (row_frame - jnp.arange(DEC_CHUNK)[None, :]).astype(F32)[None]
    new_frame = jnp.arange(NEW_PAD)[None, :]
    visible = ((past_len + new_frame) // CHUNK) <= ((past_len + row_frame) // CHUNK)
    bias_new = jnp.where(((new_frame < t) & visible)[None],
                         slope * jnp.abs(row_frame - new_frame).astype(F32)[None], -NEG_BIG)
    whole = lambda bi, c: (bi, 0, 0)
    fixed2 = lambda bi, c: (0, 0)
    fixed3 = lambda bi, c: (0, 0, 0)
    frames = pl.BlockSpec((1, t, D_MODEL), whole)
    padded = pl.BlockSpec((1, NEW_PAD, D_MODEL), whole)
    chunk = pl.BlockSpec((1, DEC_CHUNK * H_A, HEAD_COLS), lambda bi, c: (bi, c, 0))
    return pl.pallas_call(
        functools.partial(_diff_decode_kernel, lam_init=lam_init, past_len=past_len),
        grid=(b, past_len // DEC_CHUNK),
        in_specs=[pl.BlockSpec((8, HEAD_COLS), fixed2), pl.BlockSpec((1, HEAD_COLS), fixed2),
                  pl.BlockSpec(slope.shape, fixed3), pl.BlockSpec(bias_past.shape, fixed3),
                  pl.BlockSpec(bias_new.shape, fixed3), frames, chunk, chunk, padded, padded, frames],
        out_specs=frames,
        out_shape=jax.ShapeDtypeStruct((b, t, D_MODEL), BF16),
        scratch_shapes=[pltpu.VMEM((n_half, rows, 1), F32), pltpu.VMEM((n_half, rows, 1), F32),
                        pltpu.VMEM((n_half, rows, DEC_WIDE), F32)],
        compiler_params=_cparams(2),
        name="diff_attn_decode",
    )(lam_rows, subln_g.reshape(1, HEAD_COLS).astype(F32), slope, bias_past, bias_new, q,
      cache_k.reshape(b, past_len * H_A, HEAD_COLS), cache_v.reshape(b, past_len * H_A, HEAD_COLS),
      _pad_new(kb_new), _pad_new(vb_new), sg)


def _sb_decode_attention(q, cache_k, cache_v, kb_new, vb_new, sg):
    b, t, _ = q.shape
    past_len = cache_k.shape[1]
    cache_k = cache_k.reshape(b, past_len, D_MODEL).astype(BF16)
    cache_v = cache_v.reshape(b, past_len, D_MODEL).astype(BF16)
    grp = lambda bi, h: (bi, 0, h)
    qblk = pl.BlockSpec((1, t, DEC_WIDE), grp)
    cblk = pl.BlockSpec((1, past_len, DEC_WIDE), grp)
    nblk = pl.BlockSpec((1, NEW_PAD, DEC_WIDE), grp)
    return pl.pallas_call(
        functools.partial(_sb_decode_kernel, past_len=past_len, n_new=t),
        grid=(b, D_MODEL // DEC_WIDE),
        in_specs=[pl.BlockSpec((TK, TK), lambda bi, h: (0, 0)), qblk, cblk, cblk, nblk, nblk, qblk],
        out_specs=qblk,
        out_shape=jax.ShapeDtypeStruct((b, t, D_MODEL), BF16),
        compiler_params=_cparams(2),
        name="sb_attn_decode",
    )(_later_key_matrix(TK), q, cache_k, cache_v, _pad_new(kb_new), _pad_new(vb_new), sg)


def _prompt_layer(x, proj, w_out, attend, v_blocks):
    b, s, _ = x.shape
    q, k, v, kb, vb, sg = _in_proj(x, *proj[:3], tm=TK, v_blocks=v_blocks, q_scale=proj[3])
    a = attend(q, kb, vb, sg)
    y = _out_proj(x.reshape(b * s, D_MODEL), a.reshape(b * s, D_MODEL), w_out, tm=512)
    return y.reshape(b, s, D_MODEL), k, v


def _decode_layer(x, cache_k, cache_v, proj, w_out, attend):
    b, t, _ = x.shape
    n = b * t
    outs = _in_proj(x.reshape(1, n, D_MODEL), *proj[:3], tm=n, v_blocks=False, q_scale=proj[3])
    q, k, v, kb, vb, sg = [o.reshape(b, t, D_MODEL) for o in outs]
    a = attend(q, cache_k, cache_v, kb, vb, sg)
    y = _out_proj(x.reshape(n, D_MODEL), a.reshape(n, D_MODEL), w_out, tm=n)
    return y.reshape(b, t, D_MODEL), k, v


def kernel(x_prompt, x_sample, cache_k_0, cache_v_0, cache_k_1, cache_v_1, norm_g_0, w_in_0, q_norm_0, k_norm_0, lambda_q1_0, lambda_k1_0, lambda_q2_0, lambda_k2_0, subln_g_0, w_out_0, norm_g_1, w_in_1, w_out_1):
    bp, sp, _ = x_prompt.shape
    bs, ss, _ = x_sample.shape

    lam_init = 0.8 - 0.6 * math.exp(-0.3 * 0)
    lam_rows = jnp.zeros((8, HEAD_COLS), F32)
    for r, vec in enumerate((lambda_q1_0, lambda_k1_0, lambda_q2_0, lambda_k2_0)):
        lam_rows = lam_rows.at[r, 0:DH].set(vec.astype(F32))

    diff_prompt = functools.partial(_diff_prompt_attention, lam_rows=lam_rows, subln_g=subln_g_0,
                                    lam_init=lam_init)
    diff_decode = functools.partial(_diff_decode_attention, lam_rows=lam_rows, subln_g=subln_g_0,
                                    lam_init=lam_init)
    w_out_0b, w_out_1b = w_out_0.astype(BF16), w_out_1.astype(BF16)
    proj0 = (norm_g_0, w_in_0.astype(BF16), (q_norm_0, k_norm_0), QK_SCALE * LOG2_E)
    proj1 = (norm_g_1, w_in_1.astype(BF16), None, QK_SCALE * LOG2_E)

    yp, k0p, v0p = _prompt_layer(x_prompt, proj0, w_out_0b, diff_prompt, v_blocks=True)
    ys, k0s, v0s = _decode_layer(x_sample, cache_k_0, cache_v_0, proj0, w_out_0b, diff_decode)
    yp, k1p, v1p = _prompt_layer(yp, proj1, w_out_1b, _sb_prompt_attention, v_blocks=False)
    ys, k1s, v1s = _decode_layer(ys, cache_k_1, cache_v_1, proj1, w_out_1b, _sb_decode_attention)

    return (yp, ys,
            k0p.reshape(bp, sp, H_A, 2 * DH), v0p.reshape(bp, sp, H_A, 2 * DH),
            k0s.reshape(bs, ss, H_A, 2 * DH), v0s.reshape(bs, ss, H_A, 2 * DH),
            k1p.reshape(bp, sp, 2 * H_A, DH), v1p.reshape(bp, sp, 2 * H_A, DH),
            k1s.reshape(bs, ss, 2 * H_A, DH), v1s.reshape(bs, ss, 2 * H_A, DH))
```

```python
import functools
import math

import jax
import jax.numpy as jnp
from jax import lax
from jax.experimental import pallas as pl
from jax.experimental.pallas import tpu as pltpu

F32 = jnp.float32
BF16 = jnp.bfloat16

D_MODEL = 1024
CHUNK = 64
H_A = 8
DH = 64
HEAD_COLS = 128
N_GROUPS = D_MODEL // HEAD_COLS
RMS_EPS = 1e-6
NEG_BIG = -1e30
BF16_ROWS = 16
QK_SCALE = DH ** -0.5
LOG2_E = math.log2(math.e)

TQ = 256
TK = 256
NEW_PAD = 128
SB_GROUPS = 2
DIFF_GROUPS = 4
DEC_CHUNK = 512
DEC_WIDE = 512

VMEM_LIMIT = 48 * 1024 * 1024


def _cparams(n_axes):
    return pltpu.CompilerParams(
        dimension_semantics=("arbitrary",) * n_axes, vmem_limit_bytes=VMEM_LIMIT)


def _dot(a, b):
    return jnp.dot(a, b, preferred_element_type=F32)


def _dot_nt(a, b):
    return lax.dot_general(a, b, (((1,), (1,)), ((), ())), preferred_element_type=F32)


def _group_rms(t, gain_ref, bd_ref):
    sq = (t * t).astype(BF16)
    outs = []
    for c in range(0, D_MODEL, 256):
        ms = _dot(sq[:, c:c + 256], bd_ref[...])
        outs.append(t[:, c:c + 256] * lax.rsqrt(ms + RMS_EPS) * gain_ref[:, c:c + 256])
    return jnp.concatenate(outs, axis=1)


def _inproj_kernel(*refs, qk_norm, v_blocks, q_scale):
    if qk_norm:
        (x_ref, g_ref, w_ref, qg_ref, kg_ref, bd_ref,
         q_ref, k_ref, v_ref, kb_ref, vb_ref, sg_ref) = refs
    else:
        x_ref, g_ref, w_ref, q_ref, k_ref, v_ref, kb_ref, vb_ref, sg_ref = refs
    x = x_ref[0]
    ms = jnp.mean(x * x, axis=-1, keepdims=True)
    h = (x * lax.rsqrt(ms + RMS_EPS) * g_ref[...]).astype(BF16)

    q = _dot(h, w_ref[:, 0:D_MODEL])
    if qk_norm:
        q = _group_rms(q, qg_ref, bd_ref)
    q_ref[0] = (q * q_scale).astype(BF16)

    k = _dot(h, w_ref[:, D_MODEL:2 * D_MODEL])
    if qk_norm:
        k = _group_rms(k, kg_ref, bd_ref)
    k_ref[0] = k
    kb_ref[0] = k.astype(BF16)

    v = _dot(h, w_ref[:, 2 * D_MODEL:3 * D_MODEL])
    v_ref[0] = v
    if v_blocks:
        for g in range(N_GROUPS):
            vb_ref[0, g, 0] = v[:, g * HEAD_COLS:(g + 1) * HEAD_COLS].T.astype(BF16)
    else:
        vb_ref[0] = v.astype(BF16)

    gate = _dot(h, w_ref[:, 3 * D_MODEL:4 * D_MODEL])
    sg_ref[0] = (gate / (1.0 + jnp.exp(-gate))).astype(BF16)


def _in_proj(x, norm_g, w_bf16, qk_gains, tm, v_blocks, q_scale):
    b, s, _ = x.shape
    row = lambda bi, si: (bi, si, 0)
    fixed = lambda bi, si: (0, 0)
    in_specs = [pl.BlockSpec((1, tm, D_MODEL), row),
                pl.BlockSpec((1, D_MODEL), fixed),
                pl.BlockSpec((D_MODEL, 4 * D_MODEL), fixed)]
    args = [x, norm_g.reshape(1, D_MODEL), w_bf16]
    if qk_gains is not None:
        q_gain, k_gain = qk_gains
        r = jnp.arange(256)
        bd = jnp.where((r[:, None] // DH) == (r[None, :] // DH), 1.0 / DH, 0.0).astype(BF16)
        in_specs += [pl.BlockSpec((1, D_MODEL), fixed), pl.BlockSpec((1, D_MODEL), fixed),
                     pl.BlockSpec((256, 256), fixed)]
        args += [jnp.tile(q_gain.astype(F32), D_MODEL // DH).reshape(1, D_MODEL),
                 jnp.tile(k_gain.astype(F32), D_MODEL // DH).reshape(1, D_MODEL), bd]
    out_block = pl.BlockSpec((1, tm, D_MODEL), row)
    shp = lambda dt: jax.ShapeDtypeStruct((b, s, D_MODEL), dt)
    if v_blocks:
        vb_block = pl.BlockSpec((1, N_GROUPS, 1, HEAD_COLS, tm), lambda bi, si: (bi, 0, si, 0, 0))
        vb_shape = jax.ShapeDtypeStruct((b, N_GROUPS, s // tm, HEAD_COLS, tm), BF16)
    else:
        vb_block, vb_shape = out_block, shp(BF16)
    return pl.pallas_call(
        functools.partial(_inproj_kernel, qk_norm=qk_gains is not None, v_blocks=v_blocks, q_scale=q_scale),
        grid=(b, s // tm),
        in_specs=in_specs,
        out_specs=[out_block] * 4 + [vb_block, out_block],
        out_shape=[shp(BF16), shp(F32), shp(F32), shp(BF16), vb_shape, shp(BF16)],
        compiler_params=_cparams(2),
        name="in_proj_qknorm" if qk_gains is not None else "in_proj",
    )(*args)


def _outproj_kernel(x_ref, a_ref, w_ref, y_ref):
    y_ref[...] = x_ref[...] + _dot(a_ref[...], w_ref[...])


def _out_proj(x2d, a_bf16, w_bf16, tm):
    n = x2d.shape[0]
    row = lambda i: (i, 0)
    return pl.pallas_call(
        _outproj_kernel,
        grid=(n // tm,),
        in_specs=[pl.BlockSpec((tm, D_MODEL), row), pl.BlockSpec((tm, D_MODEL), row),
                  pl.BlockSpec((D_MODEL, D_MODEL), lambda i: (0, 0))],
        out_specs=pl.BlockSpec((tm, D_MODEL), row),
        out_shape=jax.ShapeDtypeStruct((n, D_MODEL), F32),
        compiler_params=_cparams(1),
        name="out_proj",
    )(x2d, a_bf16, w_bf16)


def _split_halves(q):
    lane = lax.broadcasted_iota(jnp.int32, q.shape, 1)
    zero = jnp.zeros_like(q)
    return jnp.concatenate([jnp.where(lane < DH, q, zero), jnp.where(lane >= DH, q, zero)], axis=0)


def _diff_lambda(lam_ref, lam_init):
    t1 = jnp.sum(lam_ref[0:1, :] * lam_ref[1:2, :], axis=-1, keepdims=True)
    t2 = jnp.sum(lam_ref[2:3, :] * lam_ref[3:4, :], axis=-1, keepdims=True)
    return jnp.exp(t1) - jnp.exp(t2) + lam_init


def _sb_terms(z2):
    lp = jnp.log2(1.0 + jnp.exp2(-jnp.abs(z2)))
    log_beta = jnp.minimum(z2, 0.0) - lp
    return log_beta, log_beta - z2


def _split_bf16(x):
    hi = x.astype(BF16)
    return hi, (x - hi.astype(F32)).astype(BF16)


def _diff_prompt_kernel(slope_ref, lam_ref, subg_ref, boff_ref, bdiag_ref,
                        q_ref, k_ref, vt_ref, sg_ref, o_ref,
                        m_ref, l_ref, acc_ref, *, lam_init):
    heads = range(DIFF_GROUPS)
    cols = lambda g: slice(g * HEAD_COLS, (g + 1) * HEAD_COLS)
    slopes = [slope_ref[pl.program_id(1) * DIFF_GROUPS + g] for g in heads]
    lam = _diff_lambda(lam_ref, lam_init)

    def q_block(qi, carry):
        q0 = pl.multiple_of(qi * TQ, TQ)
        q = q_ref[0, pl.ds(q0, TQ), :]
        qq = [_split_halves(q[:, cols(g)]) for g in heads]
        m_ref[...] = jnp.full(m_ref.shape, NEG_BIG, F32)
        l_ref[...] = jnp.zeros(l_ref.shape, F32)
        acc_ref[...] = jnp.zeros(acc_ref.shape, F32)

        def blocks(units):
            work = [(u, g) for u in range(len(units)) for g in heads]
            keys = [k_ref[0, pl.ds(pl.multiple_of(kj * TK, TK), TK), :] for kj, _, _ in units]
            score = lambda u, g: _dot_nt(keys[u][:, cols(g)], qq[g])
            ahead = DIFF_GROUPS
            ones = jnp.ones((BF16_ROWS, TK), BF16)
            scores = {i: score(*work[i]) for i in range(ahead)}
            for i, (u, g) in enumerate(work):
                kj, bias_ref, block_dist = units[u]
                bias = bias_ref[g]
                s = scores.pop(i)
                s = jnp.concatenate([s[:, 0:TQ] - bias, s[:, TQ:2 * TQ] - bias], axis=1)
                shift = slopes[g] * block_dist
                m_prev = m_ref[g]
                m_new = jnp.maximum(m_prev, jnp.max(s, axis=0, keepdims=True) - shift)
                p = jnp.exp2(s - (m_new + shift))
                alpha = jnp.exp2(m_prev - m_new)
                m_ref[g] = m_new
                pv = _dot(jnp.concatenate([vt_ref[0, g, kj], ones], axis=0), p.astype(BF16))
                acc_ref[g] = alpha * acc_ref[g] + pv[0:HEAD_COLS]
                l_ref[g] = alpha * l_ref[g] + pv[HEAD_COLS:HEAD_COLS + 1]
                if i + ahead < len(work):
                    scores[i + ahead] = score(*work[i + ahead])

        def past_pair(pair, c):
            kj = 2 * pair
            blocks([(kj, boff_ref, (qi - kj) * TQ), (kj + 1, boff_ref, (qi - kj - 1) * TQ)])
            return c

        lax.fori_loop(0, qi // 2, past_pair, 0)

        @pl.when(qi % 2 == 1)
        def _():
            blocks([(qi - 1, boff_ref, TQ), (qi, bdiag_ref, 0)])

        @pl.when(qi % 2 == 0)
        def _():
            blocks([(qi, bdiag_ref, 0)])

        outs = []
        for g in heads:
            inv_l = 1.0 / l_ref[g]
            o_t = (acc_ref[g, :, 0:TQ] * inv_l[:, 0:TQ]
                   - lam * (acc_ref[g, :, TQ:2 * TQ] * inv_l[:, TQ:2 * TQ]))
            ms = jnp.mean(o_t * o_t, axis=0, keepdims=True)
            outs.append((o_t * lax.rsqrt(ms + RMS_EPS)).T * subg_ref[...] * (1.0 - lam_init))
        on = jnp.concatenate(outs, axis=1)
        o_ref[0, pl.ds(q0, TQ), :] = (on * sg_ref[0, pl.ds(q0, TQ), :].astype(F32)).astype(BF16)
        return carry

    lax.fori_loop(0, q_ref.shape[1] // TQ, q_block, 0)


def _sb_prompt_kernel(u_ref, q_ref, k_ref, v_ref, sg_ref, o_ref, c_ref, acc_ref, z_ref):
    lane = lax.broadcasted_iota(jnp.int32, (TQ, HEAD_COLS), 1)
    n_heads = 2 * SB_GROUPS
    cols = lambda hd: slice((hd // 2) * HEAD_COLS, (hd // 2 + 1) * HEAD_COLS)

    def q_block(qi, carry):
        q0 = pl.multiple_of(qi * TQ, TQ)
        q = q_ref[0, pl.ds(q0, TQ), :]
        zero = jnp.zeros((TQ, HEAD_COLS), BF16)
        q_heads = [jnp.where((lane >= DH) == bool(hd % 2), q[:, cols(hd)], zero) for hd in range(n_heads)]
        c_ref[...] = jnp.zeros(c_ref.shape, F32)
        acc_ref[...] = jnp.zeros(acc_ref.shape, F32)

        def scores(kj):
            k = k_ref[0, pl.ds(pl.multiple_of(kj * TK, TK), TK), :]
            return [_dot_nt(q_heads[hd], k[:, cols(hd)]) for hd in range(n_heads)]

        def block(kj, diagonal):
            z = [z_ref[hd] for hd in range(n_heads)]
            z_next = scores(jnp.maximum(kj - 1, 0))
            v = v_ref[0, pl.ds(pl.multiple_of(kj * TK, TK), TK), :]
            if diagonal:
                qry = lax.broadcasted_iota(jnp.int32, (TQ, TK), 0)
                key = lax.broadcasted_iota(jnp.int32, (TQ, TK), 1)
                before = key < qry
            log_beta, later = [], []
            for hd in range(n_heads):
                lb, log_keep = _sb_terms(z[hd])
                if diagonal:
                    log_keep = jnp.where(before, log_keep, 0.0)
                log_beta.append(lb)
                later.append(_dot(log_keep.astype(BF16), u_ref[...]) + c_ref[hd])
                c_ref[hd] += jnp.sum(log_keep, axis=1, keepdims=True)
            for hd in range(n_heads):
                a = jnp.exp2(log_beta[hd] + later[hd])
                if diagonal:
                    a = jnp.where(before, a, 0.0)
                acc_ref[hd] += _dot(a.astype(BF16), v[:, cols(hd)])
            for hd in range(n_heads):
                z_ref[hd] = z_next[hd]

        first = scores(qi)
        for hd in range(n_heads):
            z_ref[hd] = first[hd]
        block(qi, True)

        def earlier(i, c):
            block(qi - 1 - i, False)
            return c

        lax.fori_loop(0, qi, earlier, 0)

        o = jnp.concatenate([jnp.where(lane < DH, acc_ref[2 * g], acc_ref[2 * g + 1])
                             for g in range(SB_GROUPS)], axis=1)
        o_ref[0, pl.ds(q0, TQ), :] = (o * sg_ref[0, pl.ds(q0, TQ), :].astype(F32)).astype(BF16)
        return carry

    lax.fori_loop(0, q_ref.shape[1] // TQ, q_block, 0)


def _later_key_matrix(n):
    r = jnp.arange(n)
    return (r[:, None] > r[None, :]).astype(BF16)


def _alibi_slopes_log2():
    return LOG2_E * 2.0 ** (-8.0 * jnp.arange(1, H_A + 1, dtype=F32) / H_A)


def _diff_prompt_attention(q, kb, vt, sg, lam_rows, subln_g, lam_init):
    b, s, _ = q.shape
    slopes = _alibi_slopes_log2()
    key = jnp.arange(TK)[:, None]
    qry = jnp.arange(TQ)[None, :]
    rel = (qry - key).astype(F32)
    visible = (key // CHUNK) <= (qry // CHUNK)
    bias_off = slopes[:, None, None] * rel[None]
    bias_diag = jnp.where(visible[None], slopes[:, None, None] * jnp.abs(rel)[None], -NEG_BIG)
    grp = lambda bi, h: (bi, 0, h)
    seq = pl.BlockSpec((1, s, DIFF_GROUPS * HEAD_COLS), grp)
    vtspec = pl.BlockSpec((1, DIFF_GROUPS, s // TK, HEAD_COLS, TK), lambda bi, h: (bi, h, 0, 0, 0))
    biasblk = pl.BlockSpec((DIFF_GROUPS, TK, TQ), lambda bi, h: (h, 0, 0))
    fixed2 = lambda bi, h: (0, 0)
    return pl.pallas_call(
        functools.partial(_diff_prompt_kernel, lam_init=lam_init),
        grid=(b, N_GROUPS // DIFF_GROUPS),
        in_specs=[pl.BlockSpec(memory_space=pltpu.SMEM),
                  pl.BlockSpec((8, HEAD_COLS), fixed2), pl.BlockSpec((1, HEAD_COLS), fixed2),
                  biasblk, biasblk, seq, seq, vtspec, seq],
        out_specs=seq,
        out_shape=jax.ShapeDtypeStruct((b, s, D_MODEL), BF16),
        scratch_shapes=[pltpu.VMEM((DIFF_GROUPS, 1, 2 * TQ), F32), pltpu.VMEM((DIFF_GROUPS, 1, 2 * TQ), F32),
                        pltpu.VMEM((DIFF_GROUPS, HEAD_COLS, 2 * TQ), F32)],
        compiler_params=_cparams(2),
        name="diff_attn_prompt",
    )(slopes, lam_rows, subln_g.reshape(1, HEAD_COLS).astype(F32), bias_off, bias_diag, q, kb, vt, sg)


def _sb_prompt_attention(q, kb, vb, sg):
    b, s, _ = q.shape
    seq = pl.BlockSpec((1, s, SB_GROUPS * HEAD_COLS), lambda bi, h: (bi, 0, h))
    n_heads = 2 * SB_GROUPS
    return pl.pallas_call(
        _sb_prompt_kernel,
        grid=(b, N_GROUPS // SB_GROUPS),
        in_specs=[pl.BlockSpec((TK, TK), lambda bi, h: (0, 0)), seq, seq, seq, seq],
        out_specs=seq,
        out_shape=jax.ShapeDtypeStruct((b, s, D_MODEL), BF16),
        scratch_shapes=[pltpu.VMEM((n_heads, TQ, 1), F32), pltpu.VMEM((n_heads, TQ, HEAD_COLS), F32),
                        pltpu.VMEM((n_heads, TQ, TK), F32)],
        compiler_params=_cparams(2),
        name="sb_attn_prompt",
    )(_later_key_matrix(TK), q, kb, vb, sg)


def _diff_decode_kernel(lam_ref, subg_ref, slope_ref, bias_past_ref, bias_new_ref,
                        q_ref, ck_ref, cv_ref, nk_ref, nv_ref, sg_ref, o_ref,
                        m_ref, l_ref, acc_ref, *, lam_init, past_len):
    c = pl.program_id(1)
    t = q_ref.shape[1]
    per_step = DEC_WIDE // HEAD_COLS
    n_half = H_A // per_step
    lane = lax.broadcasted_iota(jnp.int32, (t, DEC_WIDE), 1)

    @pl.when(c == 0)
    def _():
        m_ref[...] = jnp.full(m_ref.shape, NEG_BIG, F32)
        l_ref[...] = jnp.zeros(l_ref.shape, F32)
        acc_ref[...] = jnp.zeros(acc_ref.shape, F32)

    def queries(hh):
        q = q_ref[0, :, hh * DEC_WIDE:(hh + 1) * DEC_WIDE]
        zero = jnp.zeros_like(q)
        keep = lambda comp, h: (lane >= h * HEAD_COLS + comp * DH) & (lane < h * HEAD_COLS + (comp + 1) * DH)
        return jnp.concatenate([jnp.where(keep(comp, h), q, zero)
                                for comp in range(2) for h in range(per_step)], axis=0)

    def update(hh, k, v, bias, shift):
        s = _dot_nt(queries(hh), k) - bias
        m_prev = m_ref[hh]
        m_new = jnp.maximum(m_prev, jnp.max(s, axis=1, keepdims=True) - shift)
        p = jnp.exp2(s - (m_new + shift))
        alpha = jnp.exp2(m_prev - m_new)
        l_ref[hh] = alpha * l_ref[hh] + jnp.sum(p, axis=1, keepdims=True)
        acc_ref[hh] = alpha * acc_ref[hh] + _dot(p.astype(BF16), v)
        m_ref[hh] = m_new

    def cached(ref, hh):
        heads = [ref[0, pl.ds(hh * per_step + h, DEC_CHUNK, stride=H_A), :] for h in range(per_step)]
        return jnp.concatenate(heads, axis=1).astype(BF16)

    chunk_dist = past_len - c * DEC_CHUNK
    for hh in range(n_half):
        update(hh, cached(ck_ref, hh), cached(cv_ref, hh), bias_past_ref[hh], slope_ref[hh] * chunk_dist)

    @pl.when(c == pl.num_programs(1) - 1)
    def _():
        lam = _diff_lambda(lam_ref, lam_init)
        outs = []
        for hh in range(n_half):
            cols = slice(hh * DEC_WIDE, (hh + 1) * DEC_WIDE)
            update(hh, nk_ref[0, :, cols], nv_ref[0, :, cols], bias_new_ref[hh], 0.0)
            inv_l = 1.0 / l_ref[hh]
            for h in range(per_step):
                r1 = slice(h * t, (h + 1) * t)
                r2 = slice((per_step + h) * t, (per_step + h + 1) * t)
                own = slice(h * HEAD_COLS, (h + 1) * HEAD_COLS)
                o = acc_ref[hh, r1, own] * inv_l[r1] - lam * (acc_ref[hh, r2, own] * inv_l[r2])
                ms = jnp.mean(o * o, axis=-1, keepdims=True)
                outs.append(o * lax.rsqrt(ms + RMS_EPS) * subg_ref[...] * (1.0 - lam_init))
        on = jnp.concatenate(outs, axis=1)
        o_ref[0] = (on * sg_ref[0].astype(F32)).astype(BF16)


def _sb_decode_kernel(u_ref, q_ref, ck_ref, cv_ref, nk_ref, nv_ref, sg_ref, o_ref, *, past_len, n_new):
    t = q_ref.shape[1]
    n_heads = DEC_WIDE // DH
    lane = lax.broadcasted_iota(jnp.int32, (t, DEC_WIDE), 1)
    own = [(lane >= h * DH) & (lane < (h + 1) * DH) for h in range(n_heads)]
    q = q_ref[0]
    zero = jnp.zeros_like(q)
    qq = jnp.concatenate([jnp.where(own[h], q, zero) for h in range(n_heads)], axis=0)
    rows = n_heads * t

    def later_sums(log_keep, u):
        hi, lo = _split_bf16(log_keep)
        return _dot(hi, u) + _dot(lo, u)

    frame = lax.broadcasted_iota(jnp.int32, (rows, NEW_PAD), 0) & (t - 1)
    col = lax.broadcasted_iota(jnp.int32, (rows, NEW_PAD), 1)
    before = (col < frame) & (col < n_new)
    log_beta, log_keep = _sb_terms(_dot_nt(qq, nk_ref[0]))
    log_keep = jnp.where(before, log_keep, 0.0)
    a = jnp.exp2(log_beta + later_sums(log_keep, u_ref[0:NEW_PAD, 0:NEW_PAD]))
    acc = _dot(jnp.where(before, a, 0.0).astype(BF16), nv_ref[0])
    carry = jnp.sum(log_keep, axis=1, keepdims=True)

    log_beta, log_keep = _sb_terms(_dot_nt(qq, ck_ref[0].astype(BF16)))
    blocks = [slice(blk * TK, (blk + 1) * TK) for blk in range(past_len // TK)]
    later = [later_sums(log_keep[:, s], u_ref[...]) for s in blocks]
    weights = [None] * len(blocks)
    for blk in reversed(range(len(blocks))):
        weights[blk] = jnp.exp2(log_beta[:, blocks[blk]] + later[blk] + carry).astype(BF16)
        carry = carry + jnp.sum(log_keep[:, blocks[blk]], axis=1, keepdims=True)
    acc = acc + _dot(jnp.concatenate(weights, axis=1), cv_ref[0].astype(BF16))

    o = jnp.zeros((t, DEC_WIDE), F32)
    for h in range(n_heads):
        o = jnp.where(own[h], acc[h * t:(h + 1) * t, :], o)
    o_ref[0] = (o * sg_ref[0].astype(F32)).astype(BF16)


def _pad_new(x):
    return jnp.pad(x, ((0, 0), (0, NEW_PAD - x.shape[1]), (0, 0)))


def _diff_decode_attention(q, cache_k, cache_v, kb_new, vb_new, sg, lam_rows, subln_g, lam_init):
    b, t, _ = q.shape
    past_len = cache_k.shape[1]
    per_step = DEC_WIDE // HEAD_COLS
    n_half = H_A // per_step
    rows = 2 * per_step * t
    row_head = (jnp.arange(rows) // t) % per_step
    row_frame = (jnp.arange(rows) % t)[:, None]
    heads = jnp.arange(n_half)[:, None] * per_step + row_head[None, :]
    slope = _alibi_slopes_log2()[heads][:, :, None]
    bias_past = slope * (row_frame - jnp.arange(DEC_CHUNK)[None, :]).astype(F32)[None]
    new_frame = jnp.arange(NEW_PAD)[None, :]
    visible = ((past_len + new_frame) // CHUNK) <= ((past_len + row_frame) // CHUNK)
    bias_new = jnp.where(((new_frame < t) & visible)[None],
                         slope * jnp.abs(row_frame - new_frame).astype(F32)[None], -NEG_BIG)
    whole = lambda bi, c: (bi, 0, 0)
    fixed2 = lambda bi, c: (0, 0)
    fixed3 = lambda bi, c: (0, 0, 0)
    frames = pl.BlockSpec((1, t, D_MODEL), whole)
    padded = pl.BlockSpec((1, NEW_PAD, D_MODEL), whole)
    chunk = pl.BlockSpec((1, DEC_CHUNK * H_A, HEAD_COLS), lambda bi, c: (bi, c, 0))
    return pl.pallas_call(
        functools.partial(_diff_decode_kernel, lam_init=lam_init, past_len=past_len),
        grid=(b, past_len // DEC_CHUNK),
        in_specs=[pl.BlockSpec((8, HEAD_COLS), fixed2), pl.BlockSpec((1, HEAD_COLS), fixed2),
                  pl.BlockSpec(slope.shape, fixed3), pl.BlockSpec(bias_past.shape, fixed3),
                  pl.BlockSpec(bias_new.shape, fixed3), frames, chunk, chunk, padded, padded, frames],
        out_specs=frames,
        out_shape=jax.ShapeDtypeStruct((b, t, D_MODEL), BF16),
        scratch_shapes=[pltpu.VMEM((n_half, rows, 1), F32), pltpu.VMEM((n_half, rows, 1), F32),
                        pltpu.VMEM((n_half, rows, DEC_WIDE), F32)],
        compiler_params=_cparams(2),
        name="diff_attn_decode",
    )(lam_rows, subln_g.reshape(1, HEAD_COLS).astype(F32), slope, bias_past, bias_new, q,
      cache_k.reshape(b, past_len * H_A, HEAD_COLS), cache_v.reshape(b, past_len * H_A, HEAD_COLS),
      _pad_new(kb_new), _pad_new(vb_new), sg)


def _sb_decode_attention(q, cache_k, cache_v, kb_new, vb_new, sg):
    b, t, _ = q.shape
    past_len = cache_k.shape[1]
    cache_k = cache_k.reshape(b, past_len, D_MODEL)
    cache_v = cache_v.reshape(b, past_len, D_MODEL)
    grp = lambda bi, h: (bi, 0, h)
    qblk = pl.BlockSpec((1, t, DEC_WIDE), grp)
    cblk = pl.BlockSpec((1, past_len, DEC_WIDE), grp)
    nblk = pl.BlockSpec((1, NEW_PAD, DEC_WIDE), grp)
    return pl.pallas_call(
        functools.partial(_sb_decode_kernel, past_len=past_len, n_new=t),
        grid=(b, D_MODEL // DEC_WIDE),
        in_specs=[pl.BlockSpec((TK, TK), lambda bi, h: (0, 0)), qblk, cblk, cblk, nblk, nblk, qblk],
        out_specs=qblk,
        out_shape=jax.ShapeDtypeStruct((b, t, D_MODEL), BF16),
        compiler_params=_cparams(2),
        name="sb_attn_decode",
    )(_later_key_matrix(TK), q, cache_k, cache_v, _pad_new(kb_new), _pad_new(vb_new), sg)


def _prompt_layer(x, proj, w_out, attend, v_blocks):
    b, s, _ = x.shape
    q, k, v, kb, vb, sg = _in_proj(x, *proj[:3], tm=TK, v_blocks=v_blocks, q_scale=proj[3])
    a = attend(q, kb, vb, sg)
    y = _out_proj(x.reshape(b * s, D_MODEL), a.reshape(b * s, D_MODEL), w_out, tm=512)
    return y.reshape(b, s, D_MODEL), k, v


def _decode_layer(x, cache_k, cache_v, proj, w_out, attend):
    b, t, _ = x.shape
    n = b * t
    outs = _in_proj(x.reshape(1, n, D_MODEL), *proj[:3], tm=n, v_blocks=False, q_scale=proj[3])
    q, k, v, kb, vb, sg = [o.reshape(b, t, D_MODEL) for o in outs]
    a = attend(q, cache_k, cache_v, kb, vb, sg)
    y = _out_proj(x.reshape(n, D_MODEL), a.reshape(n, D_MODEL), w_out, tm=n)
    return y.reshape(b, t, D_MODEL), k, v


def kernel(x_prompt, x_sample, cache_k_0, cache_v_0, cache_k_1, cache_v_1, norm_g_0, w_in_0, q_norm_0, k_norm_0, lambda_q1_0, lambda_k1_0, lambda_q2_0, lambda_k2_0, subln_g_0, w_out_0, norm_g_1, w_in_1, w_out_1):
    bp, sp, _ = x_prompt.shape
    bs, ss, _ = x_sample.shape

    lam_init = 0.8 - 0.6 * math.exp(-0.3 * 0)
    lam_rows = jnp.zeros((8, HEAD_COLS), F32)
    for r, vec in enumerate((lambda_q1_0, lambda_k1_0, lambda_q2_0, lambda_k2_0)):
        lam_rows = lam_rows.at[r, 0:DH].set(vec.astype(F32))

    diff_prompt = functools.partial(_diff_prompt_attention, lam_rows=lam_rows, subln_g=subln_g_0,
                                    lam_init=lam_init)
    diff_decode = functools.partial(_diff_decode_attention, lam_rows=lam_rows, subln_g=subln_g_0,
                                    lam_init=lam_init)
    w_out_0b, w_out_1b = w_out_0.astype(BF16), w_out_1.astype(BF16)
    proj0 = (norm_g_0, w_in_0.astype(BF16), (q_norm_0, k_norm_0), QK_SCALE * LOG2_E)
    proj1 = (norm_g_1, w_in_1.astype(BF16), None, QK_SCALE * LOG2_E)

    yp, k0p, v0p = _prompt_layer(x_prompt, proj0, w_out_0b, diff_prompt, v_blocks=True)
    ys, k0s, v0s = _decode_layer(x_sample, cache_k_0, cache_v_0, proj0, w_out_0b, diff_decode)
    yp, k1p, v1p = _prompt_layer(yp, proj1, w_out_1b, _sb_prompt_attention, v_blocks=False)
    ys, k1s, v1s = _decode_layer(ys, cache_k_1, cache_v_1, proj1, w_out_1b, _sb_decode_attention)

    return (yp, ys,
            k0p.reshape(bp, sp, H_A, 2 * DH), v0p.reshape(bp, sp, H_A, 2 * DH),
            k0s.reshape(bs, ss, H_A, 2 * DH), v0s.reshape(bs, ss, H_A, 2 * DH),
            k1p.reshape(bp, sp, 2 * H_A, DH), v1p.reshape(bp, sp, 2 * H_A, DH),
            k1s.reshape(bs, ss, 2 * H_A, DH), v1s.reshape(bs, ss, 2 * H_A, DH))
```

```python
import functools
import math

import jax
import jax.numpy as jnp
from jax import lax
from jax.experimental import pallas as pl
from jax.experimental.pallas import tpu as pltpu

F32 = jnp.float32
BF16 = jnp.bfloat16

D_MODEL = 1024
CHUNK = 64
H_A = 8
DH = 64
HEAD_COLS = 128
N_GROUPS = D_MODEL // HEAD_COLS
RMS_EPS = 1e-6
NEG_BIG = -1e30
BF16_ROWS = 16
QK_SCALE = DH ** -0.5
LOG2_E = math.log2(math.e)

TQ = 256
TK = 256
NEW_PAD = 128
SB_GROUPS = 2
DIFF_GROUPS = 4
DEC_CHUNK = 512
DEC_WIDE = 512

VMEM_LIMIT = 48 * 1024 * 1024


def _cparams(n_axes):
    return pltpu.CompilerParams(
        dimension_semantics=("arbitrary",) * n_axes, vmem_limit_bytes=VMEM_LIMIT)


def _dot(a, b):
    return jnp.dot(a, b, preferred_element_type=F32)


def _dot_nt(a, b):
    return lax.dot_general(a, b, (((1,), (1,)), ((), ())), preferred_element_type=F32)


def _group_rms(t, gain_ref, bd_ref):
    sq = (t * t).astype(BF16)
    outs = []
    for c in range(0, D_MODEL, 256):
        ms = _dot(sq[:, c:c + 256], bd_ref[...])
        outs.append(t[:, c:c + 256] * lax.rsqrt(ms + RMS_EPS) * gain_ref[:, c:c + 256])
    return jnp.concatenate(outs, axis=1)


def _inproj_kernel(*refs, qk_norm, v_blocks, q_scale):
    if qk_norm:
        (x_ref, g_ref, w_ref, qg_ref, kg_ref, bd_ref,
         q_ref, k_ref, v_ref, kb_ref, vb_ref, sg_ref) = refs
    else:
        x_ref, g_ref, w_ref, q_ref, k_ref, v_ref, kb_ref, vb_ref, sg_ref = refs
    x = x_ref[0]
    ms = jnp.mean(x * x, axis=-1, keepdims=True)
    h = (x * lax.rsqrt(ms + RMS_EPS) * g_ref[...]).astype(BF16)

    q = _dot(h, w_ref[:, 0:D_MODEL])
    if qk_norm:
        q = _group_rms(q, qg_ref, bd_ref)
    q_ref[0] = (q * q_scale).astype(BF16)

    k = _dot(h, w_ref[:, D_MODEL:2 * D_MODEL])
    if qk_norm:
        k = _group_rms(k, kg_ref, bd_ref)
    k_ref[0] = k
    kb_ref[0] = k.astype(BF16)

    v = _dot(h, w_ref[:, 2 * D_MODEL:3 * D_MODEL])
    v_ref[0] = v
    if v_blocks:
        for g in range(N_GROUPS):
            vb_ref[0, g, 0] = v[:, g * HEAD_COLS:(g + 1) * HEAD_COLS].T.astype(BF16)
    else:
        vb_ref[0] = v.astype(BF16)

    gate = _dot(h, w_ref[:, 3 * D_MODEL:4 * D_MODEL])
    sg_ref[0] = (gate / (1.0 + jnp.exp(-gate))).astype(BF16)


def _in_proj(x, norm_g, w_bf16, qk_gains, tm, v_blocks, q_scale):
    b, s, _ = x.shape
    row = lambda bi, si: (bi, si, 0)
    fixed = lambda bi, si: (0, 0)
    in_specs = [pl.BlockSpec((1, tm, D_MODEL), row),
                pl.BlockSpec((1, D_MODEL), fixed),
                pl.BlockSpec((D_MODEL, 4 * D_MODEL), fixed)]
    args = [x, norm_g.reshape(1, D_MODEL), w_bf16]
    if qk_gains is not None:
        q_gain, k_gain = qk_gains
        r = jnp.arange(256)
        bd = jnp.where((r[:, None] // DH) == (r[None, :] // DH), 1.0 / DH, 0.0).astype(BF16)
        in_specs += [pl.BlockSpec((1, D_MODEL), fixed), pl.BlockSpec((1, D_MODEL), fixed),
                     pl.BlockSpec((256, 256), fixed)]
        args += [jnp.tile(q_gain.astype(F32), D_MODEL // DH).reshape(1, D_MODEL),
                 jnp.tile(k_gain.astype(F32), D_MODEL // DH).reshape(1, D_MODEL), bd]
    out_block = pl.BlockSpec((1, tm, D_MODEL), row)
    shp = lambda dt: jax.ShapeDtypeStruct((b, s, D_MODEL), dt)
    if v_blocks:
        vb_block = pl.BlockSpec((1, N_GROUPS, 1, HEAD_COLS, tm), lambda bi, si: (bi, 0, si, 0, 0))
        vb_shape = jax.ShapeDtypeStruct((b, N_GROUPS, s // tm, HEAD_COLS, tm), BF16)
    else:
        vb_block, vb_shape = out_block, shp(BF16)
    return pl.pallas_call(
        functools.partial(_inproj_kernel, qk_norm=qk_gains is not None, v_blocks=v_blocks, q_scale=q_scale),
        grid=(b, s // tm),
        in_specs=in_specs,
        out_specs=[out_block] * 4 + [vb_block, out_block],
        out_shape=[shp(BF16), shp(F32), shp(F32), shp(BF16), vb_shape, shp(BF16)],
        compiler_params=_cparams(2),
        name="in_proj_qknorm" if qk_gains is not None else "in_proj",
    )(*args)


def _outproj_kernel(x_ref, a_ref, w_ref, y_ref):
    y_ref[...] = x_ref[...] + _dot(a_ref[...], w_ref[...])


def _out_proj(x2d, a_bf16, w_bf16, tm):
    n = x2d.shape[0]
    row = lambda i: (i, 0)
    return pl.pallas_call(
        _outproj_kernel,
        grid=(n // tm,),
        in_specs=[pl.BlockSpec((tm, D_MODEL), row), pl.BlockSpec((tm, D_MODEL), row),
                  pl.BlockSpec((D_MODEL, D_MODEL), lambda i: (0, 0))],
        out_specs=pl.BlockSpec((tm, D_MODEL), row),
        out_shape=jax.ShapeDtypeStruct((n, D_MODEL), F32),
        compiler_params=_cparams(1),
        name="out_proj",
    )(x2d, a_bf16, w_bf16)


def _split_halves(q):
    lane = lax.broadcasted_iota(jnp.int32, q.shape, 1)
    zero = jnp.zeros_like(q)
    return jnp.concatenate([jnp.where(lane < DH, q, zero), jnp.where(lane >= DH, q, zero)], axis=0)


def _diff_lambda(lam_ref, lam_init):
    t1 = jnp.sum(lam_ref[0:1, :] * lam_ref[1:2, :], axis=-1, keepdims=True)
    t2 = jnp.sum(lam_ref[2:3, :] * lam_ref[3:4, :], axis=-1, keepdims=True)
    return jnp.exp(t1) - jnp.exp(t2) + lam_init


def _sb_terms(z2):
    lp = jnp.log2(1.0 + jnp.exp2(-jnp.abs(z2)))
    log_beta = jnp.minimum(z2, 0.0) - lp
    return log_beta, log_beta - z2


def _split_bf16(x):
    hi = x.astype(BF16)
    return hi, (x - hi.astype(F32)).astype(BF16)


def _diff_prompt_kernel(slope_ref, lam_ref, subg_ref, boff_ref, bdiag_ref,
                        q_ref, k_ref, vt_ref, sg_ref, o_ref,
                        m_ref, l_ref, acc_ref, *, lam_init):
    heads = range(DIFF_GROUPS)
    cols = lambda g: slice(g * HEAD_COLS, (g + 1) * HEAD_COLS)
    slopes = [slope_ref[pl.program_id(1) * DIFF_GROUPS + g] for g in heads]
    lam = _diff_lambda(lam_ref, lam_init)

    def q_block(qi, carry):
        q0 = pl.multiple_of(qi * TQ, TQ)
        q = q_ref[0, pl.ds(q0, TQ), :]
        qq = [_split_halves(q[:, cols(g)]) for g in heads]
        m_ref[...] = jnp.full(m_ref.shape, NEG_BIG, F32)
        l_ref[...] = jnp.zeros(l_ref.shape, F32)
        acc_ref[...] = jnp.zeros(acc_ref.shape, F32)

        def blocks(units):
            work = [(u, g) for u in range(len(units)) for g in heads]
            keys = [k_ref[0, pl.ds(pl.multiple_of(kj * TK, TK), TK), :] for kj, _, _ in units]
            score = lambda u, g: _dot_nt(keys[u][:, cols(g)], qq[g])
            ahead = DIFF_GROUPS
            ones = jnp.ones((BF16_ROWS, TK), BF16)
            scores = {i: score(*work[i]) for i in range(ahead)}
            for i, (u, g) in enumerate(work):
                kj, bias_ref, block_dist = units[u]
                bias = bias_ref[g]
                s = scores.pop(i)
                s = jnp.concatenate([s[:, 0:TQ] - bias, s[:, TQ:2 * TQ] - bias], axis=1)
                shift = slopes[g] * block_dist
                m_prev = m_ref[g]
                m_new = jnp.maximum(m_prev, jnp.max(s, axis=0, keepdims=True) - shift)
                p = jnp.exp2(s - (m_new + shift))
                alpha = jnp.exp2(m_prev - m_new)
                m_ref[g] = m_new
                pv = _dot(jnp.concatenate([vt_ref[0, g, kj], ones], axis=0), p.astype(BF16))
                acc_ref[g] = alpha * acc_ref[g] + pv[0:HEAD_COLS]
                l_ref[g] = alpha * l_ref[g] + pv[HEAD_COLS:HEAD_COLS + 1]
                if i + ahead < len(work):
                    scores[i + ahead] = score(*work[i + ahead])

        def past_pair(pair, c):
            kj = 2 * pair
            blocks([(kj, boff_ref, (qi - kj) * TQ), (kj + 1, boff_ref, (qi - kj - 1) * TQ)])
            return c

        even_tail = (qi % 2 == 0) & (qi > 0)
        lax.fori_loop(0, qi // 2 - jnp.where(even_tail, 1, 0), past_pair, 0)

        @pl.when(qi % 2 == 1)
        def _():
            blocks([(qi - 1, boff_ref, TQ), (qi, bdiag_ref, 0)])

        @pl.when(even_tail)
        def _():
            blocks([(qi - 2, boff_ref, 2 * TQ), (qi - 1, boff_ref, TQ), (qi, bdiag_ref, 0)])

        @pl.when(qi == 0)
        def _():
            blocks([(qi, bdiag_ref, 0)])

        outs = []
        for g in heads:
            inv_l = 1.0 / l_ref[g]
            o_t = (acc_ref[g, :, 0:TQ] * inv_l[:, 0:TQ]
                   - lam * (acc_ref[g, :, TQ:2 * TQ] * inv_l[:, TQ:2 * TQ]))
            ms = jnp.mean(o_t * o_t, axis=0, keepdims=True)
            outs.append((o_t * lax.rsqrt(ms + RMS_EPS)).T * subg_ref[...] * (1.0 - lam_init))
        on = jnp.concatenate(outs, axis=1)
        o_ref[0, pl.ds(q0, TQ), :] = (on * sg_ref[0, pl.ds(q0, TQ), :].astype(F32)).astype(BF16)
        return carry

    lax.fori_loop(0, q_ref.shape[1] // TQ, q_block, 0)


def _sb_prompt_kernel(u_ref, q_ref, k_ref, v_ref, sg_ref, o_ref, c_ref, acc_ref, z_ref):
    lane = lax.broadcasted_iota(jnp.int32, (TQ, HEAD_COLS), 1)
    n_heads = 2 * SB_GROUPS
    cols = lambda hd: slice((hd // 2) * HEAD_COLS, (hd // 2 + 1) * HEAD_COLS)

    def q_block(qi, carry):
        q0 = pl.multiple_of(qi * TQ, TQ)
        q = q_ref[0, pl.ds(q0, TQ), :]
        zero = jnp.zeros((TQ, HEAD_COLS), BF16)
        q_heads = [jnp.where((lane >= DH) == bool(hd % 2), q[:, cols(hd)], zero) for hd in range(n_heads)]
        c_ref[...] = jnp.zeros(c_ref.shape, F32)
        acc_ref[...] = jnp.zeros(acc_ref.shape, F32)

        def scores(kj):
            k = k_ref[0, pl.ds(pl.multiple_of(kj * TK, TK), TK), :]
            return [_dot_nt(q_heads[hd], k[:, cols(hd)]) for hd in range(n_heads)]

        def block(kj, diagonal):
            z = [z_ref[hd] for hd in range(n_heads)]
            z_next = scores(jnp.maximum(kj - 1, 0))
            v = v_ref[0, pl.ds(pl.multiple_of(kj * TK, TK), TK), :]
            if diagonal:
                qry = lax.broadcasted_iota(jnp.int32, (TQ, TK), 0)
                key = lax.broadcasted_iota(jnp.int32, (TQ, TK), 1)
                before = key < qry
            log_beta, later = [], []
            for hd in range(n_heads):
                lb, log_keep = _sb_terms(z[hd])
                if diagonal:
                    log_keep = jnp.where(before, log_keep, 0.0)
                log_beta.append(lb)
                later.append(_dot(log_keep.astype(BF16), u_ref[...]) + c_ref[hd])
                c_ref[hd] += jnp.sum(log_keep, axis=1, keepdims=True)
            for hd in range(n_heads):
                a = jnp.exp2(log_beta[hd] + later[hd])
                if diagonal:
                    a = jnp.where(before, a, 0.0)
                acc_ref[hd] += _dot(a.astype(BF16), v[:, cols(hd)])
            for hd in range(n_heads):
                z_ref[hd] = z_next[hd]

        first = scores(qi)
        for hd in range(n_heads):
            z_ref[hd] = first[hd]
        block(qi, True)

        def earlier(i, c):
            block(qi - 1 - i, False)
            return c

        lax.fori_loop(0, qi, earlier, 0)

        o = jnp.concatenate([jnp.where(lane < DH, acc_ref[2 * g], acc_ref[2 * g + 1])
                             for g in range(SB_GROUPS)], axis=1)
        o_ref[0, pl.ds(q0, TQ), :] = (o * sg_ref[0, pl.ds(q0, TQ), :].astype(F32)).astype(BF16)
        return carry

    lax.fori_loop(0, q_ref.shape[1] // TQ, q_block, 0)


def _later_key_matrix(n):
    r = jnp.arange(n)
    return (r[:, None] > r[None, :]).astype(BF16)


def _alibi_slopes_log2():
    return LOG2_E * 2.0 ** (-8.0 * jnp.arange(1, H_A + 1, dtype=F32) / H_A)


def _diff_prompt_attention(q, kb, vt, sg, lam_rows, subln_g, lam_init):
    b, s, _ = q.shape
    slopes = _alibi_slopes_log2()
    key = jnp.arange(TK)[:, None]
    qry = jnp.arange(TQ)[None, :]
    rel = (qry - key).astype(F32)
    visible = (key // CHUNK) <= (qry // CHUNK)
    bias_off = slopes[:, None, None] * rel[None]
    bias_diag = jnp.where(visible[None], slopes[:, None, None] * jnp.abs(rel)[None], -NEG_BIG)
    grp = lambda bi, h: (bi, 0, h)
    seq = pl.BlockSpec((1, s, DIFF_GROUPS * HEAD_COLS), grp)
    vtspec = pl.BlockSpec((1, DIFF_GROUPS, s // TK, HEAD_COLS, TK), lambda bi, h: (bi, h, 0, 0, 0))
    biasblk = pl.BlockSpec((DIFF_GROUPS, TK, TQ), lambda bi, h: (h, 0, 0))
    fixed2 = lambda bi, h: (0, 0)
    return pl.pallas_call(
        functools.partial(_diff_prompt_kernel, lam_init=lam_init),
        grid=(b, N_GROUPS // DIFF_GROUPS),
        in_specs=[pl.BlockSpec(memory_space=pltpu.SMEM),
                  pl.BlockSpec((8, HEAD_COLS), fixed2), pl.BlockSpec((1, HEAD_COLS), fixed2),
                  biasblk, biasblk, seq, seq, vtspec, seq],
        out_specs=seq,
        out_shape=jax.ShapeDtypeStruct((b, s, D_MODEL), BF16),
        scratch_shapes=[pltpu.VMEM((DIFF_GROUPS, 1, 2 * TQ), F32), pltpu.VMEM((DIFF_GROUPS, 1, 2 * TQ), F32),
                        pltpu.VMEM((DIFF_GROUPS, HEAD_COLS, 2 * TQ), F32)],
        compiler_params=_cparams(2),
        name="diff_attn_prompt",
    )(slopes, lam_rows, subln_g.reshape(1, HEAD_COLS).astype(F32), bias_off, bias_diag, q, kb, vt, sg)


def _sb_prompt_attention(q, kb, vb, sg):
    b, s, _ = q.shape
    seq = pl.BlockSpec((1, s, SB_GROUPS * HEAD_COLS), lambda bi, h: (bi, 0, h))
    n_heads = 2 * SB_GROUPS
    return pl.pallas_call(
        _sb_prompt_kernel,
        grid=(b, N_GROUPS // SB_GROUPS),
        in_specs=[pl.BlockSpec((TK, TK), lambda bi, h: (0, 0)), seq, seq, seq, seq],
        out_specs=seq,
        out_shape=jax.ShapeDtypeStruct((b, s, D_MODEL), BF16),
        scratch_shapes=[pltpu.VMEM((n_heads, TQ, 1), F32), pltpu.VMEM((n_heads, TQ, HEAD_COLS), F32),
                        pltpu.VMEM((n_heads, TQ, TK), F32)],
        compiler_params=_cparams(2),
        name="sb_attn_prompt",
    )(_later_key_matrix(TK), q, kb, vb, sg)


def _diff_decode_kernel(lam_ref, subg_ref, slope_ref, bias_past_ref, bias_new_ref,
                        q_ref, ck_ref, cv_ref, nk_ref, nv_ref, sg_ref, o_ref,
                        m_ref, l_ref, acc_ref, *, lam_init, past_len):
    c = pl.program_id(1)
    t = q_ref.shape[1]
    per_step = DEC_WIDE // HEAD_COLS
    n_half = H_A // per_step
    lane = lax.broadcasted_iota(jnp.int32, (t, DEC_WIDE), 1)

    @pl.when(c == 0)
    def _():
        m_ref[...] = jnp.full(m_ref.shape, NEG_BIG, F32)
        l_ref[...] = jnp.zeros(l_ref.shape, F32)
        acc_ref[...] = jnp.zeros(acc_ref.shape, F32)

    def queries(hh):
        q = q_ref[0, :, hh * DEC_WIDE:(hh + 1) * DEC_WIDE]
        zero = jnp.zeros_like(q)
        keep = lambda comp, h: (lane >= h * HEAD_COLS + comp * DH) & (lane < h * HEAD_COLS + (comp + 1) * DH)
        return jnp.concatenate([jnp.where(keep(comp, h), q, zero)
                                for comp in range(2) for h in range(per_step)], axis=0)

    def update(hh, k, v, bias, shift):
        s = _dot_nt(queries(hh), k) - bias
        m_prev = m_ref[hh]
        m_new = jnp.maximum(m_prev, jnp.max(s, axis=1, keepdims=True) - shift)
        p = jnp.exp2(s - (m_new + shift))
        alpha = jnp.exp2(m_prev - m_new)
        l_ref[hh] = alpha * l_ref[hh] + jnp.sum(p, axis=1, keepdims=True)
        acc_ref[hh] = alpha * acc_ref[hh] + _dot(p.astype(BF16), v)
        m_ref[hh] = m_new

    def cached(ref, hh):
        heads = [ref[0, pl.ds(hh * per_step + h, DEC_CHUNK, stride=H_A), :] for h in range(per_step)]
        return jnp.concatenate(heads, axis=1).astype(BF16)

    chunk_dist = past_len - c * DEC_CHUNK
    for hh in range(n_half):
        update(hh, cached(ck_ref, hh), cached(cv_ref, hh), bias_past_ref[hh], slope_ref[hh] * chunk_dist)

    @pl.when(c == pl.num_programs(1) - 1)
    def _():
        lam = _diff_lambda(lam_ref, lam_init)
        outs = []
        for hh in range(n_half):
            cols = slice(hh * DEC_WIDE, (hh + 1) * DEC_WIDE)
            update(hh, nk_ref[0, :, cols], nv_ref[0, :, cols], bias_new_ref[hh], 0.0)
            inv_l = 1.0 / l_ref[hh]
            for h in range(per_step):
                r1 = slice(h * t, (h + 1) * t)
                r2 = slice((per_step + h) * t, (per_step + h + 1) * t)
                own = slice(h * HEAD_COLS, (h + 1) * HEAD_COLS)
                o = acc_ref[hh, r1, own] * inv_l[r1] - lam * (acc_ref[hh, r2, own] * inv_l[r2])
                ms = jnp.mean(o * o, axis=-1, keepdims=True)
                outs.append(o * lax.rsqrt(ms + RMS_EPS) * subg_ref[...] * (1.0 - lam_init))
        on = jnp.concatenate(outs, axis=1)
        o_ref[0] = (on * sg_ref[0].astype(F32)).astype(BF16)


def _sb_decode_kernel(u_ref, q_ref, ck_ref, cv_ref, nk_ref, nv_ref, sg_ref, o_ref, *, past_len, n_new):
    t = q_ref.shape[1]
    n_heads = DEC_WIDE // DH
    lane = lax.broadcasted_iota(jnp.int32, (t, DEC_WIDE), 1)
    own = [(lane >= h * DH) & (lane < (h + 1) * DH) for h in range(n_heads)]
    q = q_ref[0]
    zero = jnp.zeros_like(q)
    qq = jnp.concatenate([jnp.where(own[h], q, zero) for h in range(n_heads)], axis=0)
    rows = n_heads * t

    def later_sums(log_keep, u):
        hi, lo = _split_bf16(log_keep)
        return _dot(hi, u) + _dot(lo, u)

    frame = lax.broadcasted_iota(jnp.int32, (rows, NEW_PAD), 0) & (t - 1)
    col = lax.broadcasted_iota(jnp.int32, (rows, NEW_PAD), 1)
    before = (col < frame) & (col < n_new)
    log_beta, log_keep = _sb_terms(_dot_nt(qq, nk_ref[0]))
    log_keep = jnp.where(before, log_keep, 0.0)
    a = jnp.exp2(log_beta + later_sums(log_keep, u_ref[0:NEW_PAD, 0:NEW_PAD]))
    acc = _dot(jnp.where(before, a, 0.0).astype(BF16), nv_ref[0])
    carry = jnp.sum(log_keep, axis=1, keepdims=True)

    log_beta, log_keep = _sb_terms(_dot_nt(qq, ck_ref[0].astype(BF16)))
    blocks = [slice(blk * TK, (blk + 1) * TK) for blk in range(past_len // TK)]
    later = [later_sums(log_keep[:, s], u_ref[...]) for s in blocks]
    weights = [None] * len(blocks)
    for blk in reversed(range(len(blocks))):
        weights[blk] = jnp.exp2(log_beta[:, blocks[blk]] + later[blk] + carry).astype(BF16)
        carry = carry + jnp.sum(log_keep[:, blocks[blk]], axis=1, keepdims=True)
    acc = acc + _dot(jnp.concatenate(weights, axis=1), cv_ref[0].astype(BF16))

    o = jnp.zeros((t, DEC_WIDE), F32)
    for h in range(n_heads):
        o = jnp.where(own[h], acc[h * t:(h + 1) * t, :], o)
    o_ref[0] = (o * sg_ref[0].astype(F32)).astype(BF16)


def _pad_new(x):
    return jnp.pad(x, ((0, 0), (0, NEW_PAD - x.shape[1]), (0, 0)))


def _diff_decode_attention(q, cache_k, cache_v, kb_new, vb_new, sg, lam_rows, subln_g, lam_init):
    b, t, _ = q.shape
    past_len = cache_k.shape[1]
    per_step = DEC_WIDE // HEAD_COLS
    n_half = H_A // per_step
    rows = 2 * per_step * t
    row_head = (jnp.arange(rows) // t) % per_step
    row_frame = (jnp.arange(rows) % t)[:, None]
    heads = jnp.arange(n_half)[:, None] * per_step + row_head[None, :]
    slope = _alibi_slopes_log2()[heads][:, :, None]
    bias_past = slope * (row_frame - jnp.arange(DEC_CHUNK)[None, :]).astype(F32)[None]
    new_frame = jnp.arange(NEW_PAD)[None, :]
    visible = ((past_len + new_frame) // CHUNK) <= ((past_len + row_frame) // CHUNK)
    bias_new = jnp.where(((new_frame < t) & visible)[None],
                         slope * jnp.abs(row_frame - new_frame).astype(F32)[None], -NEG_BIG)
    whole = lambda bi, c: (bi, 0, 0)
    fixed2 = lambda bi, c: (0, 0)
    fixed3 = lambda bi, c: (0, 0, 0)
    frames = pl.BlockSpec((1, t, D_MODEL), whole)
    padded = pl.BlockSpec((1, NEW_PAD, D_MODEL), whole)
    chunk = pl.BlockSpec((1, DEC_CHUNK * H_A, HEAD_COLS), lambda bi, c: (bi, c, 0))
    return pl.pallas_call(
        functools.partial(_diff_decode_kernel, lam_init=lam_init, past_len=past_len),
        grid=(b, past_len // DEC_CHUNK),
        in_specs=[pl.BlockSpec((8, HEAD_COLS), fixed2), pl.BlockSpec((1, HEAD_COLS), fixed2),
                  pl.BlockSpec(slope.shape, fixed3), pl.BlockSpec(bias_past.shape, fixed3),
                  pl.BlockSpec(bias_new.shape, fixed3), frames, chunk, chunk, padded, padded, frames],
        out_specs=frames,
        out_shape=jax.ShapeDtypeStruct((b, t, D_MODEL), BF16),
        scratch_shapes=[pltpu.VMEM((n_half, rows, 1), F32), pltpu.VMEM((n_half, rows, 1), F32),
                        pltpu.VMEM((n_half, rows, DEC_WIDE), F32)],
        compiler_params=_cparams(2),
        name="diff_attn_decode",
    )(lam_rows, subln_g.reshape(1, HEAD_COLS).astype(F32), slope, bias_past, bias_new, q,
      cache_k.reshape(b, past_len * H_A, HEAD_COLS), cache_v.reshape(b, past_len * H_A, HEAD_COLS),
      _pad_new(kb_new), _pad_new(vb_new), sg)


def _sb_decode_attention(q, cache_k, cache_v, kb_new, vb_new, sg):
    b, t, _ = q.shape
    past_len = cache_k.shape[1]
    cache_k = cache_k.reshape(b, past_len, D_MODEL)
    cache_v = cache_v.reshape(b, past_len, D_MODEL)
    grp = lambda bi, h: (bi, 0, h)
    qblk = pl.BlockSpec((1, t, DEC_WIDE), grp)
    cblk = pl.BlockSpec((1, past_len, DEC_WIDE), grp)
    nblk = pl.BlockSpec((1, NEW_PAD, DEC_WIDE), grp)
    return pl.pallas_call(
        functools.partial(_sb_decode_kernel, past_len=past_len, n_new=t),
        grid=(b, D_MODEL // DEC_WIDE),
        in_specs=[pl.BlockSpec((TK, TK), lambda bi, h: (0, 0)), qblk, cblk, cblk, nblk, nblk, qblk],
        out_specs=qblk,
        out_shape=jax.ShapeDtypeStruct((b, t, D_MODEL), BF16),
        compiler_params=_cparams(2),
        name="sb_attn_decode",
    )(_later_key_matrix(TK), q, cache_k, cache_v, _pad_new(kb_new), _pad_new(vb_new), sg)


def _prompt_layer(x, proj, w_out, attend, v_blocks):
    b, s, _ = x.shape
    q, k, v, kb, vb, sg = _in_proj(x, *proj[:3], tm=TK, v_blocks=v_blocks, q_scale=proj[3])
    a = attend(q, kb, vb, sg)
    y = _out_proj(x.reshape(b * s, D_MODEL), a.reshape(b * s, D_MODEL), w_out, tm=512)
    return y.reshape(b, s, D_MODEL), k, v


def _decode_layer(x, cache_k, cache_v, proj, w_out, attend):
    b, t, _ = x.shape
    n = b * t
    outs = _in_proj(x.reshape(1, n, D_MODEL), *proj[:3], tm=n, v_blocks=False, q_scale=proj[3])
    q, k, v, kb, vb, sg = [o.reshape(b, t, D_MODEL) for o in outs]
    a = attend(q, cache_k, cache_v, kb, vb, sg)
    y = _out_proj(x.reshape(n, D_MODEL), a.reshape(n, D_MODEL), w_out, tm=n)
    return y.reshape(b, t, D_MODEL), k, v


def kernel(x_prompt, x_sample, cache_k_0, cache_v_0, cache_k_1, cache_v_1, norm_g_0, w_in_0, q_norm_0, k_norm_0, lambda_q1_0, lambda_k1_0, lambda_q2_0, lambda_k2_0, subln_g_0, w_out_0, norm_g_1, w_in_1, w_out_1):
    bp, sp, _ = x_prompt.shape
    bs, ss, _ = x_sample.shape

    lam_init = 0.8 - 0.6 * math.exp(-0.3 * 0)
    lam_rows = jnp.zeros((8, HEAD_COLS), F32)
    for r, vec in enumerate((lambda_q1_0, lambda_k1_0, lambda_q2_0, lambda_k2_0)):
        lam_rows = lam_rows.at[r, 0:DH].set(vec.astype(F32))

    diff_prompt = functools.partial(_diff_prompt_attention, lam_rows=lam_rows, subln_g=subln_g_0,
                                    lam_init=lam_init)
    diff_decode = functools.partial(_diff_decode_attention, lam_rows=lam_rows, subln_g=subln_g_0,
                                    lam_init=lam_init)
    w_out_0b, w_out_1b = w_out_0.astype(BF16), w_out_1.astype(BF16)
    proj0 = (norm_g_0, w_in_0.astype(BF16), (q_norm_0, k_norm_0), QK_SCALE * LOG2_E)
    proj1 = (norm_g_1, w_in_1.astype(BF16), None, QK_SCALE * LOG2_E)

    yp, k0p, v0p = _prompt_layer(x_prompt, proj0, w_out_0b, diff_prompt, v_blocks=True)
    ys, k0s, v0s = _decode_layer(x_sample, cache_k_0, cache_v_0, proj0, w_out_0b, diff_decode)
    yp, k1p, v1p = _prompt_layer(yp, proj1, w_out_1b, _sb_prompt_attention, v_blocks=False)
    ys, k1s, v1s = _decode_layer(ys, cache_k_1, cache_v_1, proj1, w_out_1b, _sb_decode_attention)

    return (yp, ys,
            k0p.reshape(bp, sp, H_A, 2 * DH), v0p.reshape(bp, sp, H_A, 2 * DH),
            k0s.reshape(bs, ss, H_A, 2 * DH), v0s.reshape(bs, ss, H_A, 2 * DH),
            k1p.reshape(bp, sp, 2 * H_A, DH), v1p.reshape(bp, sp, 2 * H_A, DH),
            k1s.reshape(bs, ss, 2 * H_A, DH), v1s.reshape(bs, ss, 2 * H_A, DH))
```

```python
import functools
import math

import jax
import jax.numpy as jnp
from jax import lax
from jax.experimental import pallas as pl
from jax.experimental.pallas import tpu as pltpu

F32 = jnp.float32
BF16 = jnp.bfloat16

D_MODEL = 1024
CHUNK = 64
H_A = 8
DH = 64
HEAD_COLS = 128
N_GROUPS = D_MODEL // HEAD_COLS
RMS_EPS = 1e-6
NEG_BIG = -1e30
BF16_ROWS = 16
QK_SCALE = DH ** -0.5
LOG2_E = math.log2(math.e)

TQ = 256
TK = 256
NEW_PAD = 128
SB_GROUPS = 2
DIFF_GROUPS = 4
DEC_CHUNK = 512
DEC_WIDE = 512

VMEM_LIMIT = 48 * 1024 * 1024


def _cparams(n_axes):
    return pltpu.CompilerParams(
        dimension_semantics=("arbitrary",) * n_axes, vmem_limit_bytes=VMEM_LIMIT)


def _dot(a, b):
    return jnp.dot(a, b, preferred_element_type=F32)


def _dot_nt(a, b):
    return lax.dot_general(a, b, (((1,), (1,)), ((), ())), preferred_element_type=F32)


def _group_rms(t, gain_ref, bd_ref):
    sq = (t * t).astype(BF16)
    outs = []
    for c in range(0, D_MODEL, 256):
        ms = _dot(sq[:, c:c + 256], bd_ref[...])
        outs.append(t[:, c:c + 256] * lax.rsqrt(ms + RMS_EPS) * gain_ref[:, c:c + 256])
    return jnp.concatenate(outs, axis=1)


def _inproj_kernel(*refs, qk_norm, v_blocks, q_scale):
    if qk_norm:
        (x_ref, g_ref, w_ref, qg_ref, kg_ref, bd_ref,
         q_ref, k_ref, v_ref, kb_ref, vb_ref, sg_ref) = refs
    else:
        x_ref, g_ref, w_ref, q_ref, k_ref, v_ref, kb_ref, vb_ref, sg_ref = refs
    x = x_ref[0]
    ms = jnp.mean(x * x, axis=-1, keepdims=True)
    h = (x * lax.rsqrt(ms + RMS_EPS) * g_ref[...]).astype(BF16)

    q = _dot(h, w_ref[:, 0:D_MODEL])
    if qk_norm:
        q = _group_rms(q, qg_ref, bd_ref)
    q_ref[0] = (q * q_scale).astype(BF16)

    k = _dot(h, w_ref[:, D_MODEL:2 * D_MODEL])
    if qk_norm:
        k = _group_rms(k, kg_ref, bd_ref)
    k_ref[0] = k
    kb_ref[0] = k.astype(BF16)

    v = _dot(h, w_ref[:, 2 * D_MODEL:3 * D_MODEL])
    v_ref[0] = v
    if v_blocks:
        for g in range(N_GROUPS):
            vb_ref[0, g, 0] = v[:, g * HEAD_COLS:(g + 1) * HEAD_COLS].T.astype(BF16)
    else:
        vb_ref[0] = v.astype(BF16)

    gate = _dot(h, w_ref[:, 3 * D_MODEL:4 * D_MODEL])
    sg_ref[0] = (gate / (1.0 + jnp.exp(-gate))).astype(BF16)


def _in_proj(x, norm_g, w_bf16, qk_gains, tm, v_blocks, q_scale):
    b, s, _ = x.shape
    row = lambda bi, si: (bi, si, 0)
    fixed = lambda bi, si: (0, 0)
    in_specs = [pl.BlockSpec((1, tm, D_MODEL), row),
                pl.BlockSpec((1, D_MODEL), fixed),
                pl.BlockSpec((D_MODEL, 4 * D_MODEL), fixed)]
    args = [x, norm_g.reshape(1, D_MODEL), w_bf16]
    if qk_gains is not None:
        q_gain, k_gain = qk_gains
        r = jnp.arange(256)
        bd = jnp.where((r[:, None] // DH) == (r[None, :] // DH), 1.0 / DH, 0.0).astype(BF16)
        in_specs += [pl.BlockSpec((1, D_MODEL), fixed), pl.BlockSpec((1, D_MODEL), fixed),
                     pl.BlockSpec((256, 256), fixed)]
        args += [jnp.tile(q_gain.astype(F32), D_MODEL // DH).reshape(1, D_MODEL),
                 jnp.tile(k_gain.astype(F32), D_MODEL // DH).reshape(1, D_MODEL), bd]
    out_block = pl.BlockSpec((1, tm, D_MODEL), row)
    shp = lambda dt: jax.ShapeDtypeStruct((b, s, D_MODEL), dt)
    if v_blocks:
        vb_block = pl.BlockSpec((1, N_GROUPS, 1, HEAD_COLS, tm), lambda bi, si: (bi, 0, si, 0, 0))
        vb_shape = jax.ShapeDtypeStruct((b, N_GROUPS, s // tm, HEAD_COLS, tm), BF16)
    else:
        vb_block, vb_shape = out_block, shp(BF16)
    return pl.pallas_call(
        functools.partial(_inproj_kernel, qk_norm=qk_gains is not None, v_blocks=v_blocks, q_scale=q_scale),
        grid=(b, s // tm),
        in_specs=in_specs,
        out_specs=[out_block] * 4 + [vb_block, out_block],
        out_shape=[shp(BF16), shp(F32), shp(F32), shp(BF16), vb_shape, shp(BF16)],
        compiler_params=_cparams(2),
        name="in_proj_qknorm" if qk_gains is not None else "in_proj",
    )(*args)


def _outproj_kernel(x_ref, a_ref, w_ref, y_ref):
    y_ref[...] = x_ref[...] + _dot(a_ref[...], w_ref[...])


def _out_proj(x2d, a_bf16, w_bf16, tm):
    n = x2d.shape[0]
    row = lambda i: (i, 0)
    if n // tm > 2:
        def streamed(x_hbm, a_hbm, w_ref, y_hbm):
            deep = pl.BlockSpec((tm, D_MODEL), row, pipeline_mode=pl.Buffered(3))
            pltpu.emit_pipeline(
                lambda x_ref, a_ref, y_ref: _outproj_kernel(x_ref, a_ref, w_ref, y_ref),
                grid=(n // tm,), in_specs=[deep, deep], out_specs=[pl.BlockSpec((tm, D_MODEL), row)],
            )(x_hbm, a_hbm, y_hbm)

        return pl.pallas_call(
            streamed,
            in_specs=[pl.BlockSpec(memory_space=pl.ANY), pl.BlockSpec(memory_space=pl.ANY),
                      pl.BlockSpec(memory_space=pltpu.VMEM)],
            out_specs=pl.BlockSpec(memory_space=pl.ANY),
            out_shape=jax.ShapeDtypeStruct((n, D_MODEL), F32),
            compiler_params=pltpu.CompilerParams(vmem_limit_bytes=VMEM_LIMIT),
            name="out_proj",
        )(x2d, a_bf16, w_bf16)
    return pl.pallas_call(
        _outproj_kernel,
        grid=(n // tm,),
        in_specs=[pl.BlockSpec((tm, D_MODEL), row), pl.BlockSpec((tm, D_MODEL), row),
                  pl.BlockSpec((D_MODEL, D_MODEL), lambda i: (0, 0))],
        out_specs=pl.BlockSpec((tm, D_MODEL), row),
        out_shape=jax.ShapeDtypeStruct((n, D_MODEL), F32),
        compiler_params=_cparams(1),
        name="out_proj",
    )(x2d, a_bf16, w_bf16)


def _split_halves(q):
    lane = lax.broadcasted_iota(jnp.int32, q.shape, 1)
    zero = jnp.zeros_like(q)
    return jnp.concatenate([jnp.where(lane < DH, q, zero), jnp.where(lane >= DH, q, zero)], axis=0)


def _diff_lambda(lam_ref, lam_init):
    t1 = jnp.sum(lam_ref[0:1, :] * lam_ref[1:2, :], axis=-1, keepdims=True)
    t2 = jnp.sum(lam_ref[2:3, :] * lam_ref[3:4, :], axis=-1, keepdims=True)
    return jnp.exp(t1) - jnp.exp(t2) + lam_init


def _sb_terms(z2):
    lp = jnp.log2(1.0 + jnp.exp2(-jnp.abs(z2)))
    log_beta = jnp.minimum(z2, 0.0) - lp
    return log_beta, log_beta - z2


def _split_bf16(x):
    hi = x.astype(BF16)
    return hi, (x - hi.astype(F32)).astype(BF16)


def _diff_prompt_kernel(slope_ref, lam_ref, subg_ref, boff_ref, bdiag_ref,
                        q_ref, k_ref, vt_ref, sg_ref, o_ref,
                        m_ref, l_ref, acc_ref, *, lam_init):
    heads = range(DIFF_GROUPS)
    cols = lambda g: slice(g * HEAD_COLS, (g + 1) * HEAD_COLS)
    slopes = [slope_ref[pl.program_id(1) * DIFF_GROUPS + g] for g in heads]
    lam = _diff_lambda(lam_ref, lam_init)

    def q_block(qi, carry):
        q0 = pl.multiple_of(qi * TQ, TQ)
        q = q_ref[0, pl.ds(q0, TQ), :]
        qq = [_split_halves(q[:, cols(g)]) for g in heads]
        m_ref[...] = jnp.full(m_ref.shape, NEG_BIG, F32)
        l_ref[...] = jnp.zeros(l_ref.shape, F32)
        acc_ref[...] = jnp.zeros(acc_ref.shape, F32)

        def blocks(units):
            work = [(u, g) for u in range(len(units)) for g in heads]
            keys = [k_ref[0, pl.ds(pl.multiple_of(kj * TK, TK), TK), :] for kj, _, _ in units]
            score = lambda u, g: _dot_nt(keys[u][:, cols(g)], qq[g])
            ahead = DIFF_GROUPS
            ones = jnp.ones((BF16_ROWS, TK), BF16)
            scores = {i: score(*work[i]) for i in range(ahead)}
            for i, (u, g) in enumerate(work):
                kj, bias_ref, block_dist = units[u]
                bias = bias_ref[g]
                s = scores.pop(i)
                s = jnp.concatenate([s[:, 0:TQ] - bias, s[:, TQ:2 * TQ] - bias], axis=1)
                shift = slopes[g] * block_dist
                m_prev = m_ref[g]
                m_new = jnp.maximum(m_prev, jnp.max(s, axis=0, keepdims=True) - shift)
                p = jnp.exp2(s - (m_new + shift))
                alpha = jnp.exp2(m_prev - m_new)
                m_ref[g] = m_new
                pv = _dot(jnp.concatenate([vt_ref[0, g, kj], ones], axis=0), p.astype(BF16))
                acc_ref[g] = alpha * acc_ref[g] + pv[0:HEAD_COLS]
                l_ref[g] = alpha * l_ref[g] + pv[HEAD_COLS:HEAD_COLS + 1]
                if i + ahead < len(work):
                    scores[i + ahead] = score(*work[i + ahead])

        def past_pair(pair, c):
            kj = 2 * pair
            blocks([(kj, boff_ref, (qi - kj) * TQ), (kj + 1, boff_ref, (qi - kj - 1) * TQ)])
            return c

        even_tail = (qi % 2 == 0) & (qi > 0)
        lax.fori_loop(0, qi // 2 - jnp.where(even_tail, 1, 0), past_pair, 0)

        @pl.when(qi % 2 == 1)
        def _():
            blocks([(qi - 1, boff_ref, TQ), (qi, bdiag_ref, 0)])

        @pl.when(even_tail)
        def _():
            blocks([(qi - 2, boff_ref, 2 * TQ), (qi - 1, boff_ref, TQ), (qi, bdiag_ref, 0)])

        @pl.when(qi == 0)
        def _():
            blocks([(qi, bdiag_ref, 0)])

        outs = []
        for g in heads:
            inv_l = 1.0 / l_ref[g]
            o_t = (acc_ref[g, :, 0:TQ] * inv_l[:, 0:TQ]
                   - lam * (acc_ref[g, :, TQ:2 * TQ] * inv_l[:, TQ:2 * TQ]))
            ms = jnp.mean(o_t * o_t, axis=0, keepdims=True)
            outs.append((o_t * lax.rsqrt(ms + RMS_EPS)).T * subg_ref[...] * (1.0 - lam_init))
        on = jnp.concatenate(outs, axis=1)
        o_ref[0, pl.ds(q0, TQ), :] = (on * sg_ref[0, pl.ds(q0, TQ), :].astype(F32)).astype(BF16)
        return carry

    lax.fori_loop(0, q_ref.shape[1] // TQ, q_block, 0)


def _sb_prompt_kernel(u_ref, q_ref, k_ref, v_ref, sg_ref, o_ref, c_ref, acc_ref, z_ref):
    lane = lax.broadcasted_iota(jnp.int32, (TQ, HEAD_COLS), 1)
    n_heads = 2 * SB_GROUPS
    cols = lambda hd: slice((hd // 2) * HEAD_COLS, (hd // 2 + 1) * HEAD_COLS)

    def q_block(qi, carry):
        q0 = pl.multiple_of(qi * TQ, TQ)
        q = q_ref[0, pl.ds(q0, TQ), :]
        zero = jnp.zeros((TQ, HEAD_COLS), BF16)
        q_heads = [jnp.where((lane >= DH) == bool(hd % 2), q[:, cols(hd)], zero) for hd in range(n_heads)]
        c_ref[...] = jnp.zeros(c_ref.shape, F32)
        acc_ref[...] = jnp.zeros(acc_ref.shape, F32)

        def scores(kj):
            k = k_ref[0, pl.ds(pl.multiple_of(kj * TK, TK), TK), :]
            return [_dot_nt(q_heads[hd], k[:, cols(hd)]) for hd in range(n_heads)]

        def block(kj, diagonal):
            z = [z_ref[hd] for hd in range(n_heads)]
            z_next = scores(jnp.maximum(kj - 1, 0))
            v = v_ref[0, pl.ds(pl.multiple_of(kj * TK, TK), TK), :]
            if diagonal:
                qry = lax.broadcasted_iota(jnp.int32, (TQ, TK), 0)
                key = lax.broadcasted_iota(jnp.int32, (TQ, TK), 1)
                before = key < qry
            log_beta, later = [], []
            for hd in range(n_heads):
                lb, log_keep = _sb_terms(z[hd])
                if diagonal:
                    log_keep = jnp.where(before, log_keep, 0.0)
                log_beta.append(lb)
                later.append(_dot(log_keep.astype(BF16), u_ref[...]) + c_ref[hd])
                c_ref[hd] += jnp.sum(log_keep, axis=1, keepdims=True)
            for hd in range(n_heads):
                a = jnp.exp2(log_beta[hd] + later[hd])
                if diagonal:
                    a = jnp.where(before, a, 0.0)
                acc_ref[hd] += _dot(a.astype(BF16), v[:, cols(hd)])
            for hd in range(n_heads):
                z_ref[hd] = z_next[hd]

        first = scores(qi)
        for hd in range(n_heads):
            z_ref[hd] = first[hd]
        block(qi, True)

        def earlier(i, c):
            block(qi - 1 - i, False)
            return c

        lax.fori_loop(0, qi, earlier, 0)

        o = jnp.concatenate([jnp.where(lane < DH, acc_ref[2 * g], acc_ref[2 * g + 1])
                             for g in range(SB_GROUPS)], axis=1)
        o_ref[0, pl.ds(q0, TQ), :] = (o * sg_ref[0, pl.ds(q0, TQ), :].astype(F32)).astype(BF16)
        return carry

    lax.fori_loop(0, q_ref.shape[1] // TQ, q_block, 0)


def _later_key_matrix(n):
    r = jnp.arange(n)
    return (r[:, None] > r[None, :]).astype(BF16)


def _alibi_slopes_log2():
    return LOG2_E * 2.0 ** (-8.0 * jnp.arange(1, H_A + 1, dtype=F32) / H_A)


def _diff_prompt_attention(q, kb, vt, sg, lam_rows, subln_g, lam_init):
    b, s, _ = q.shape
    slopes = _alibi_slopes_log2()
    key = jnp.arange(TK)[:, None]
    qry = jnp.arange(TQ)[None, :]
    rel = (qry - key).astype(F32)
    visible = (key // CHUNK) <= (qry // CHUNK)
    bias_off = slopes[:, None, None] * rel[None]
    bias_diag = jnp.where(visible[None], slopes[:, None, None] * jnp.abs(rel)[None], -NEG_BIG)
    grp = lambda bi, h: (bi, 0, h)
    seq = pl.BlockSpec((1, s, DIFF_GROUPS * HEAD_COLS), grp)
    vtspec = pl.BlockSpec((1, DIFF_GROUPS, s // TK, HEAD_COLS, TK), lambda bi, h: (bi, h, 0, 0, 0))
    biasblk = pl.BlockSpec((DIFF_GROUPS, TK, TQ), lambda bi, h: (h, 0, 0))
    fixed2 = lambda bi, h: (0, 0)
    return pl.pallas_call(
        functools.partial(_diff_prompt_kernel, lam_init=lam_init),
        grid=(b, N_GROUPS // DIFF_GROUPS),
        in_specs=[pl.BlockSpec(memory_space=pltpu.SMEM),
                  pl.BlockSpec((8, HEAD_COLS), fixed2), pl.BlockSpec((1, HEAD_COLS), fixed2),
                  biasblk, biasblk, seq, seq, vtspec, seq],
        out_specs=seq,
        out_shape=jax.ShapeDtypeStruct((b, s, D_MODEL), BF16),
        scratch_shapes=[pltpu.VMEM((DIFF_GROUPS, 1, 2 * TQ), F32), pltpu.VMEM((DIFF_GROUPS, 1, 2 * TQ), F32),
                        pltpu.VMEM((DIFF_GROUPS, HEAD_COLS, 2 * TQ), F32)],
        compiler_params=_cparams(2),
        name="diff_attn_prompt",
    )(slopes, lam_rows, subln_g.reshape(1, HEAD_COLS).astype(F32), bias_off, bias_diag, q, kb, vt, sg)


def _sb_prompt_attention(q, kb, vb, sg):
    b, s, _ = q.shape
    seq = pl.BlockSpec((1, s, SB_GROUPS * HEAD_COLS), lambda bi, h: (bi, 0, h))
    n_heads = 2 * SB_GROUPS
    return pl.pallas_call(
        _sb_prompt_kernel,
        grid=(b, N_GROUPS // SB_GROUPS),
        in_specs=[pl.BlockSpec((TK, TK), lambda bi, h: (0, 0)), seq, seq, seq, seq],
        out_specs=seq,
        out_shape=jax.ShapeDtypeStruct((b, s, D_MODEL), BF16),
        scratch_shapes=[pltpu.VMEM((n_heads, TQ, 1), F32), pltpu.VMEM((n_heads, TQ, HEAD_COLS), F32),
                        pltpu.VMEM((n_heads, TQ, TK), F32)],
        compiler_params=_cparams(2),
        name="sb_attn_prompt",
    )(_later_key_matrix(TK), q, kb, vb, sg)


def _diff_decode_kernel(lam_ref, subg_ref, slope_ref, bias_past_ref, bias_new_ref,
                        q_ref, ck_ref, cv_ref, nk_ref, nv_ref, sg_ref, o_ref,
                        m_ref, l_ref, acc_ref, *, lam_init, past_len):
    c = pl.program_id(1)
    t = q_ref.shape[1]
    per_step = DEC_WIDE // HEAD_COLS
    n_half = H_A // per_step
    lane = lax.broadcasted_iota(jnp.int32, (t, DEC_WIDE), 1)

    @pl.when(c == 0)
    def _():
        m_ref[...] = jnp.full(m_ref.shape, NEG_BIG, F32)
        l_ref[...] = jnp.zeros(l_ref.shape, F32)
        acc_ref[...] = jnp.zeros(acc_ref.shape, F32)

    def queries(hh):
        q = q_ref[0, :, hh * DEC_WIDE:(hh + 1) * DEC_WIDE]
        zero = jnp.zeros_like(q)
        keep = lambda comp, h: (lane >= h * HEAD_COLS + comp * DH) & (lane < h * HEAD_COLS + (comp + 1) * DH)
        return jnp.concatenate([jnp.where(keep(comp, h), q, zero)
                                for comp in range(2) for h in range(per_step)], axis=0)

    def update(hh, k, v, bias, shift):
        s = _dot_nt(queries(hh), k) - bias
        m_prev = m_ref[hh]
        m_new = jnp.maximum(m_prev, jnp.max(s, axis=1, keepdims=True) - shift)
        p = jnp.exp2(s - (m_new + shift))
        alpha = jnp.exp2(m_prev - m_new)
        l_ref[hh] = alpha * l_ref[hh] + jnp.sum(p, axis=1, keepdims=True)
        acc_ref[hh] = alpha * acc_ref[hh] + _dot(p.astype(BF16), v)
        m_ref[hh] = m_new

    def cached(ref, hh):
        heads = [ref[0, pl.ds(hh * per_step + h, DEC_CHUNK, stride=H_A), :] for h in range(per_step)]
        return jnp.concatenate(heads, axis=1).astype(BF16)

    chunk_dist = past_len - c * DEC_CHUNK
    for hh in range(n_half):
        update(hh, cached(ck_ref, hh), cached(cv_ref, hh), bias_past_ref[hh], slope_ref[hh] * chunk_dist)

    @pl.when(c == pl.num_programs(1) - 1)
    def _():
        lam = _diff_lambda(lam_ref, lam_init)
        outs = []
        for hh in range(n_half):
            cols = slice(hh * DEC_WIDE, (hh + 1) * DEC_WIDE)
            update(hh, nk_ref[0, :, cols], nv_ref[0, :, cols], bias_new_ref[hh], 0.0)
            inv_l = 1.0 / l_ref[hh]
            for h in range(per_step):
                r1 = slice(h * t, (h + 1) * t)
                r2 = slice((per_step + h) * t, (per_step + h + 1) * t)
                own = slice(h * HEAD_COLS, (h + 1) * HEAD_COLS)
                o = acc_ref[hh, r1, own] * inv_l[r1] - lam * (acc_ref[hh, r2, own] * inv_l[r2])
                ms = jnp.mean(o * o, axis=-1, keepdims=True)
                outs.append(o * lax.rsqrt(ms + RMS_EPS) * subg_ref[...] * (1.0 - lam_init))
        on = jnp.concatenate(outs, axis=1)
        o_ref[0] = (on * sg_ref[0].astype(F32)).astype(BF16)


def _sb_decode_kernel(u_ref, q_ref, ck_ref, cv_ref, nk_ref, nv_ref, sg_ref, o_ref, *, past_len, n_new):
    t = q_ref.shape[1]
    n_heads = DEC_WIDE // DH
    lane = lax.broadcasted_iota(jnp.int32, (t, DEC_WIDE), 1)
    own = [(lane >= h * DH) & (lane < (h + 1) * DH) for h in range(n_heads)]
    q = q_ref[0]
    zero = jnp.zeros_like(q)
    qq = jnp.concatenate([jnp.where(own[h], q, zero) for h in range(n_heads)], axis=0)
    rows = n_heads * t

    def later_sums(log_keep, u):
        hi, lo = _split_bf16(log_keep)
        return _dot(hi, u) + _dot(lo, u)

    frame = lax.broadcasted_iota(jnp.int32, (rows, NEW_PAD), 0) & (t - 1)
    col = lax.broadcasted_iota(jnp.int32, (rows, NEW_PAD), 1)
    before = (col < frame) & (col < n_new)
    log_beta, log_keep = _sb_terms(_dot_nt(qq, nk_ref[0]))
    log_keep = jnp.where(before, log_keep, 0.0)
    a = jnp.exp2(log_beta + later_sums(log_keep, u_ref[0:NEW_PAD, 0:NEW_PAD]))
    acc = _dot(jnp.where(before, a, 0.0).astype(BF16), nv_ref[0])
    carry = jnp.sum(log_keep, axis=1, keepdims=True)

    log_beta, log_keep = _sb_terms(_dot_nt(qq, ck_ref[0].astype(BF16)))
    blocks = [slice(blk * TK, (blk + 1) * TK) for blk in range(past_len // TK)]
    later = [later_sums(log_keep[:, s], u_ref[...]) for s in blocks]
    weights = [None] * len(blocks)
    for blk in reversed(range(len(blocks))):
        weights[blk] = jnp.exp2(log_beta[:, blocks[blk]] + later[blk] + carry).astype(BF16)
        carry = carry + jnp.sum(log_keep[:, blocks[blk]], axis=1, keepdims=True)
    acc = acc + _dot(jnp.concatenate(weights, axis=1), cv_ref[0].astype(BF16))

    o = jnp.zeros((t, DEC_WIDE), F32)
    for h in range(n_heads):
        o = jnp.where(own[h], acc[h * t:(h + 1) * t, :], o)
    o_ref[0] = (o * sg_ref[0].astype(F32)).astype(BF16)


def _pad_new(x):
    return jnp.pad(x, ((0, 0), (0, NEW_PAD - x.shape[1]), (0, 0)))


def _diff_decode_attention(q, cache_k, cache_v, kb_new, vb_new, sg, lam_rows, subln_g, lam_init):
    b, t, _ = q.shape
    past_len = cache_k.shape[1]
    per_step = DEC_WIDE // HEAD_COLS
    n_half = H_A // per_step
    rows = 2 * per_step * t
    row_head = (jnp.arange(rows) // t) % per_step
    row_frame = (jnp.arange(rows) % t)[:, None]
    heads = jnp.arange(n_half)[:, None] * per_step + row_head[None, :]
    slope = _alibi_slopes_log2()[heads][:, :, None]
    bias_past = slope * (row_frame - jnp.arange(DEC_CHUNK)[None, :]).astype(F32)[None]
    new_frame = jnp.arange(NEW_PAD)[None, :]
    visible = ((past_len + new_frame) // CHUNK) <= ((past_len + row_frame) // CHUNK)
    bias_new = jnp.where(((new_frame < t) & visible)[None],
                         slope * jnp.abs(row_frame - new_frame).astype(F32)[None], -NEG_BIG)
    whole = lambda bi, c: (bi, 0, 0)
    fixed2 = lambda bi, c: (0, 0)
    fixed3 = lambda bi, c: (0, 0, 0)
    frames = pl.BlockSpec((1, t, D_MODEL), whole)
    padded = pl.BlockSpec((1, NEW_PAD, D_MODEL), whole)
    chunk = pl.BlockSpec((1, DEC_CHUNK * H_A, HEAD_COLS), lambda bi, c: (bi, c, 0))
    return pl.pallas_call(
        functools.partial(_diff_decode_kernel, lam_init=lam_init, past_len=past_len),
        grid=(b, past_len // DEC_CHUNK),
        in_specs=[pl.BlockSpec((8, HEAD_COLS), fixed2), pl.BlockSpec((1, HEAD_COLS), fixed2),
                  pl.BlockSpec(slope.shape, fixed3), pl.BlockSpec(bias_past.shape, fixed3),
                  pl.BlockSpec(bias_new.shape, fixed3), frames, chunk, chunk, padded, padded, frames],
        out_specs=frames,
        out_shape=jax.ShapeDtypeStruct((b, t, D_MODEL), BF16),
        scratch_shapes=[pltpu.VMEM((n_half, rows, 1), F32), pltpu.VMEM((n_half, rows, 1), F32),
                        pltpu.VMEM((n_half, rows, DEC_WIDE), F32)],
        compiler_params=_cparams(2),
        name="diff_attn_decode",
    )(lam_rows, subln_g.reshape(1, HEAD_COLS).astype(F32), slope, bias_past, bias_new, q,
      cache_k.reshape(b, past_len * H_A, HEAD_COLS), cache_v.reshape(b, past_len * H_A, HEAD_COLS),
      _pad_new(kb_new), _pad_new(vb_new), sg)


def _sb_decode_attention(q, cache_k, cache_v, kb_new, vb_new, sg):
    b, t, _ = q.shape
    past_len = cache_k.shape[1]
    cache_k = cache_k.reshape(b, past_len, D_MODEL)
    cache_v = cache_v.reshape(b, past_len, D_MODEL)
    grp = lambda bi, h: (bi, 0, h)
    qblk = pl.BlockSpec((1, t, DEC_WIDE), grp)
    cblk = pl.BlockSpec((1, past_len, DEC_WIDE), grp)
    nblk = pl.BlockSpec((1, NEW_PAD, DEC_WIDE), grp)
    return pl.pallas_call(
        functools.partial(_sb_decode_kernel, past_len=past_len, n_new=t),
        grid=(b, D_MODEL // DEC_WIDE),
        in_specs=[pl.BlockSpec((TK, TK), lambda bi, h: (0, 0)), qblk, cblk, cblk, nblk, nblk, qblk],
        out_specs=qblk,
        out_shape=jax.ShapeDtypeStruct((b, t, D_MODEL), BF16),
        compiler_params=_cparams(2),
        name="sb_attn_decode",
    )(_later_key_matrix(TK), q, cache_k, cache_v, _pad_new(kb_new), _pad_new(vb_new), sg)


def _prompt_layer(x, proj, w_out, attend, v_blocks):
    b, s, _ = x.shape
    q, k, v, kb, vb, sg = _in_proj(x, *proj[:3], tm=TK, v_blocks=v_blocks, q_scale=proj[3])
    a = attend(q, kb, vb, sg)
    y = _out_proj(x.reshape(b * s, D_MODEL), a.reshape(b * s, D_MODEL), w_out, tm=512)
    return y.reshape(b, s, D_MODEL), k, v


def _decode_layer(x, cache_k, cache_v, proj, w_out, attend):
    b, t, _ = x.shape
    n = b * t
    outs = _in_proj(x.reshape(1, n, D_MODEL), *proj[:3], tm=n, v_blocks=False, q_scale=proj[3])
    q, k, v, kb, vb, sg = [o.reshape(b, t, D_MODEL) for o in outs]
    a = attend(q, cache_k, cache_v, kb, vb, sg)
    y = _out_proj(x.reshape(n, D_MODEL), a.reshape(n, D_MODEL), w_out, tm=n)
    return y.reshape(b, t, D_MODEL), k, v


def kernel(x_prompt, x_sample, cache_k_0, cache_v_0, cache_k_1, cache_v_1, norm_g_0, w_in_0, q_norm_0, k_norm_0, lambda_q1_0, lambda_k1_0, lambda_q2_0, lambda_k2_0, subln_g_0, w_out_0, norm_g_1, w_in_1, w_out_1):
    bp, sp, _ = x_prompt.shape
    bs, ss, _ = x_sample.shape

    lam_init = 0.8 - 0.6 * math.exp(-0.3 * 0)
    lam_rows = jnp.zeros((8, HEAD_COLS), F32)
    for r, vec in enumerate((lambda_q1_0, lambda_k1_0, lambda_q2_0, lambda_k2_0)):
        lam_rows = lam_rows.at[r, 0:DH].set(vec.astype(F32))

    diff_prompt = functools.partial(_diff_prompt_attention, lam_rows=lam_rows, subln_g=subln_g_0,
                                    lam_init=lam_init)
    diff_decode = functools.partial(_diff_decode_attention, lam_rows=lam_rows, subln_g=subln_g_0,
                                    lam_init=lam_init)
    w_out_0b, w_out_1b = w_out_0.astype(BF16), w_out_1.astype(BF16)
    proj0 = (norm_g_0, w_in_0.astype(BF16), (q_norm_0, k_norm_0), QK_SCALE * LOG2_E)
    proj1 = (norm_g_1, w_in_1.astype(BF16), None, QK_SCALE * LOG2_E)

    yp, k0p, v0p = _prompt_layer(x_prompt, proj0, w_out_0b, diff_prompt, v_blocks=True)
    ys, k0s, v0s = _decode_layer(x_sample, cache_k_0, cache_v_0, proj0, w_out_0b, diff_decode)
    yp, k1p, v1p = _prompt_layer(yp, proj1, w_out_1b, _sb_prompt_attention, v_blocks=False)
    ys, k1s, v1s = _decode_layer(ys, cache_k_1, cache_v_1, proj1, w_out_1b, _sb_decode_attention)

    return (yp, ys,
            k0p.reshape(bp, sp, H_A, 2 * DH), v0p.reshape(bp, sp, H_A, 2 * DH),
            k0s.reshape(bs, ss, H_A, 2 * DH), v0s.reshape(bs, ss, H_A, 2 * DH),
            k1p.reshape(bp, sp, 2 * H_A, DH), v1p.reshape(bp, sp, 2 * H_A, DH),
            k1s.reshape(bs, ss, 2 * H_A, DH), v1s.reshape(bs, ss, 2 * H_A, DH))
```
